```python
import math
import jax, jax.numpy as jnp
from jax import lax
import numpy as np

D_MODEL = 1024
BATCH = 4
SEQ = 8192
DEPTH = 1

CTX_LEN = 256
GRID_W = 64
N_HEADS = 8
QK_NOPE = 64
QK_ROPE = 32
QK_DIM = QK_NOPE + QK_ROPE
V_DIM = 64
Q_LORA = 384
KV_LORA = 256
ATTN_WIDTH = N_HEADS * V_DIM
ROPE_THETA = 10000.0
ROPE_AXIS_PAIRS = QK_ROPE // 4
Q_BLOCK = 128
SSM_WIDTH = 512
SSM_GROUP = 16
SSM_GROUPS = SSM_WIDTH // SSM_GROUP
SSM_STATE = 64
FFN_HIDDEN = 2816
CONV_W = 3
N_BRANCH = 2
EPS = 1e-6
IN_SPLITS = (Q_LORA, Q_LORA + KV_LORA, Q_LORA + KV_LORA + QK_ROPE,
             Q_LORA + KV_LORA + QK_ROPE + SSM_WIDTH)
IN_WIDTH = Q_LORA + KV_LORA + QK_ROPE + SSM_WIDTH + N_BRANCH * D_MODEL

kernel_name = "hybrid_mla_s5_convffn_prefix_ctx"


def rmsnorm(x, g):
    xf = x.astype(jnp.float32)
    y = xf * lax.rsqrt(jnp.mean(xf * xf, axis=-1, keepdims=True) + EPS)
    return (y * g.astype(jnp.float32)).astype(x.dtype)


def modulate(h, shift, scale):
    return h * (1.0 + scale) + shift


def axial_rope(rows):
    row = jnp.repeat(jnp.arange(rows), GRID_W)
    col = jnp.tile(jnp.arange(GRID_W), rows)
    freqs = ROPE_THETA ** (-jnp.arange(ROPE_AXIS_PAIRS, dtype=jnp.float32) / ROPE_AXIS_PAIRS)
    ang = jnp.concatenate([row[:, None] * freqs, col[:, None] * freqs], axis=-1)
    return jnp.cos(ang), jnp.sin(ang)


def apply_rope(t, cos, sin):
    nope, rope = t[..., :QK_NOPE], t[..., QK_NOPE:]
    r1, r2 = jnp.split(rope, 2, axis=-1)
    cs = cos[:, None, :].astype(t.dtype)
    sn = sin[:, None, :].astype(t.dtype)
    return jnp.concatenate([nope, r1 * cs - r2 * sn, r1 * sn + r2 * cs], axis=-1)


def mla_q(cq, p, cos, sin):
    b, n = cq.shape[:2]
    q = (rmsnorm(cq, p["q_a_g"]) @ p["w_uq"]).reshape(b, n, N_HEADS, QK_DIM)
    q = rmsnorm(q, p["q_norm_g"])
    return q if cos is None else apply_rope(q, cos, sin)


def mla_kv(ckv, krope, p, cos, sin):
    b, n = ckv.shape[:2]
    kv = (rmsnorm(ckv, p["kv_a_g"]) @ p["w_ukv"]).reshape(b, n, N_HEADS, QK_NOPE + V_DIM)
    k_nope, v = jnp.split(kv, [QK_NOPE], axis=-1)
    k_pe = jnp.broadcast_to(krope[:, :, None, :], (b, n, N_HEADS, QK_ROPE))
    k = rmsnorm(jnp.concatenate([k_nope, k_pe], axis=-1), p["k_norm_g"])
    if cos is not None:
        k = apply_rope(k, cos, sin)
    return k, v


def attend(q, k, v):
    s = jnp.einsum("bqhd,bkhd->bhqk", q, k, preferred_element_type=jnp.float32) * (QK_DIM ** -0.5)
    w = jax.nn.softmax(s, axis=-1).astype(v.dtype)
    return jnp.einsum("bhqk,bkhd->bqhd", w, v)


def latent_attention(q, k_lat, v_lat, k_ctx, v_ctx):
    k = jnp.concatenate([k_lat, k_ctx], axis=1)
    v = jnp.concatenate([v_lat, v_ctx], axis=1)
    b, n = q.shape[:2]
    qb = q.reshape(b, n // Q_BLOCK, Q_BLOCK, N_HEADS, QK_DIM).swapaxes(0, 1)
    o = lax.map(lambda qi: attend(qi, k, v), qb)
    return o.swapaxes(0, 1).reshape(b, n, ATTN_WIDTH)


def _ssm_combine(e_i, e_j):
    a_i, b_i = e_i
    a_j, b_j = e_j
    return a_j * a_i, a_j * b_i + b_j


def s5_states(u, p, init_f, init_b):
    b, n = u.shape[:2]
    ug = u.astype(jnp.float32).reshape(b, n, SSM_GROUPS, SSM_GROUP)
    bmat = lax.complex(p["b_re"].astype(jnp.float32), p["b_im"].astype(jnp.float32))
    out = []
    for sfx, init, reverse in (("f", init_f, False), ("b", init_b, True)):
        lam = lax.complex(p["lam_re_" + sfx].astype(jnp.float32), p["lam_im_" + sfx].astype(jnp.float32))
        dt = jnp.exp(p["log_dt_" + sfx].astype(jnp.float32))[:, None]
        lam_bar = jnp.exp(lam * dt)
        b_bar = ((lam_bar - 1.0) / lam)[..., None] * bmat
        bu = jnp.einsum("gnc,bsgc->bsgn", b_bar, ug)
        if init is not None:
            edge = n - 1 if reverse else 0
            bu = bu.at[:, edge].add(lam_bar * init)
        a = jnp.broadcast_to(lam_bar, bu.shape)
        _, xs = lax.associative_scan(_ssm_combine, (a, bu), reverse=reverse, axis=1)
        out.append(xs)
    return out[0], out[1]


def s5_readout(u, xs_f, xs_b, p):
    b, n = u.shape[:2]
    y = u.astype(jnp.float32) * p["d_skip"].astype(jnp.float32)
    for sfx, xs in (("f", xs_f), ("b", xs_b)):
        cm = lax.complex(p["c_re_" + sfx].astype(jnp.float32), p["c_im_" + sfx].astype(jnp.float32))
        y = y + jnp.einsum("gcn,bsgn->bsgc", cm, xs).real.reshape(b, n, SSM_WIDTH)
    return y.astype(u.dtype)


def ssm_glu(y, w_glu):
    val, gate = jnp.split(jax.nn.gelu(y) @ w_glu, 2, axis=-1)
    return val * jax.nn.sigmoid(gate)


def merge_branches(a, s, gate_logits, w_out):
    ga, gs = jnp.split(gate_logits, N_BRANCH, axis=-1)
    return (jax.nn.sigmoid(ga) * a + jax.nn.sigmoid(gs) * s) @ w_out


def dwconv3(u, w, bias):
    up = jnp.pad(u, ((0, 0), (1, 1), (0, 0)))
    return up[:, :-2] * w[0] + up[:, 1:-1] * w[1] + up[:, 2:] * w[2] + bias


def conv_ffn(h, p):
    u = dwconv3(h @ p["w_up"], p["conv_w"], p["conv_b"])
    val, gate = jnp.split(u, 2, axis=-1)
    return (jax.nn.silu(gate) * val) @ p["w_down"]


def hybrid_layer(x, ctx, c, c_ctx, p, cos, sin, update_ctx):
    mod = jax.nn.silu(c) @ p["w_mod"] + p["b_mod"]
    mod_ctx = jax.nn.silu(c_ctx) @ p["w_mod"] + p["b_mod"]
    sh1, sc1, g1, sh2, sc2, g2 = jnp.split(mod[:, None, :], 6, axis=-1)
    csh1, csc1, cg1, csh2, csc2, cg2 = jnp.split(mod_ctx, 6, axis=-1)

    h = modulate(rmsnorm(x, p["norm1_g"]), sh1, sc1)
    hc = modulate(rmsnorm(ctx, p["norm1_g"]), csh1, csc1)
    cq, ckv, kr, u, gl = jnp.split(h @ p["w_in"], IN_SPLITS, axis=-1)
    ccq, cckv, ckr, cu, cgl = jnp.split(hc @ p["w_in"], IN_SPLITS, axis=-1)

    k_c, v_c = mla_kv(cckv, ckr, p, None, None)
    xs_cf, xs_cb = s5_states(cu, p, None, None)

    q_l = mla_q(cq, p, cos, sin)
    k_l, v_l = mla_kv(ckv, kr, p, cos, sin)
    a_l = latent_attention(q_l, k_l, v_l, k_c, v_c) @ p["w_o_attn"]
    xs_f, xs_b = s5_states(u, p, xs_cf[:, -1], xs_cb[:, 0])
    s_l = ssm_glu(s5_readout(u, xs_f, xs_b, p), p["w_glu"])
    x = x + g1 * merge_branches(a_l, s_l, gl, p["w_out"])

    if update_ctx:
        a_c = attend(mla_q(ccq, p, None, None), k_c, v_c).reshape(ctx.shape[0], ctx.shape[1], ATTN_WIDTH)
        s_c = ssm_glu(s5_readout(cu, xs_cf, xs_cb, p), p["w_glu"])
        ctx = ctx + cg1 * merge_branches(a_c @ p["w_o_attn"], s_c, cgl, p["w_out"])

    x = x + g2 * conv_ffn(modulate(rmsnorm(x, p["norm2_g"]), sh2, sc2), p)
    if update_ctx:
        ctx = ctx + cg2 * conv_ffn(modulate(rmsnorm(ctx, p["norm2_g"]), csh2, csc2), p)
    return x, ctx


def setup_inputs(seed: int = 0) -> dict:
    key = jax.random.key(seed)
    ks = iter(jax.random.split(key, 48))
    L, D, G, N, F = DEPTH, D_MODEL, SSM_GROUPS, SSM_STATE, FFN_HIDDEN

    def nrm(shape, scale):
        return jax.random.normal(next(ks), shape, jnp.float32) * scale

    def gain(n):
        return 1.0 + nrm((L, n), 0.02)

    n_idx = jnp.arange(N, dtype=jnp.float32)
    out = {}
    out["x"] = nrm((BATCH, SEQ, D), 1.0)
    out["c"] = nrm((BATCH, D), 1.0)
    out["ctx"] = nrm((BATCH, CTX_LEN, D), 1.0)
    out["c_ctx"] = nrm((D,), 1.0)
    out["w_mod"] = nrm((L, D, 6 * D), 0.5 * D ** -0.5)
    out["b_mod"] = nrm((L, 6 * D), 0.01)
    out["norm1_g"] = gain(D)
    out["norm2_g"] = gain(D)
    out["w_in"] = nrm((L, D, IN_WIDTH), D ** -0.5)
    out["q_a_g"] = gain(Q_LORA)
    out["w_uq"] = nrm((L, Q_LORA, N_HEADS * QK_DIM), Q_LORA ** -0.5)
    out["kv_a_g"] = gain(KV_LORA)
    out["w_ukv"] = nrm((L, KV_LORA, N_HEADS * (QK_NOPE + V_DIM)), KV_LORA ** -0.5)
    out["q_norm_g"] = gain(QK_DIM)
    out["k_norm_g"] = gain(QK_DIM)
    out["w_o_attn"] = nrm((L, ATTN_WIDTH, D), ATTN_WIDTH ** -0.5)
    for sfx in ("f", "b"):
        out["lam_re_" + sfx] = -0.5 + nrm((L, G, N), 0.01)
        out["lam_im_" + sfx] = math.pi * n_idx + nrm((L, G, N), 0.01)
        out["log_dt_" + sfx] = jax.random.uniform(next(ks), (L, G), jnp.float32,
                                                  math.log(1e-3), math.log(1e-1))
        out["c_re_" + sfx] = nrm((L, G, SSM_GROUP, N), (2.0 * N) ** -0.5)
        out["c_im_" + sfx] = nrm((L, G, SSM_GROUP, N), (2.0 * N) ** -0.5)
    out["b_re"] = nrm((L, G, N, SSM_GROUP), (2.0 * SSM_GROUP) ** -0.5)
    out["b_im"] = nrm((L, G, N, SSM_GROUP), (2.0 * SSM_GROUP) ** -0.5)
    out["d_skip"] = nrm((L, SSM_WIDTH), 1.0)
    out["w_glu"] = nrm((L, SSM_WIDTH, 2 * D), SSM_WIDTH ** -0.5)
    out["w_out"] = nrm((L, D, D), D ** -0.5)
    out["w_up"] = nrm((L, D, 2 * F), D ** -0.5)
    out["conv_w"] = nrm((L, CONV_W, 2 * F), CONV_W ** -0.5)
    out["conv_b"] = nrm((L, 2 * F), 0.01)
    out["w_down"] = nrm((L, F, D), F ** -0.5)
    return out


def reference(x, c, ctx, c_ctx, w_mod, b_mod, norm1_g, norm2_g, w_in, q_a_g, w_uq, kv_a_g, w_ukv,
              q_norm_g, k_norm_g, w_o_attn,
              lam_re_f, lam_im_f, log_dt_f, c_re_f, c_im_f,
              lam_re_b, lam_im_b, log_dt_b, c_re_b, c_im_b,
              b_re, b_im, d_skip, w_glu, w_out, w_up, conv_w, conv_b, w_down):
    rows = x.shape[1] // GRID_W
    cos, sin = axial_rope(rows)
    for l in range(DEPTH):
        p = dict(w_mod=w_mod[l], b_mod=b_mod[l], norm1_g=norm1_g[l], norm2_g=norm2_g[l], w_in=w_in[l],
                 q_a_g=q_a_g[l], w_uq=w_uq[l], kv_a_g=kv_a_g[l], w_ukv=w_ukv[l],
                 q_norm_g=q_norm_g[l], k_norm_g=k_norm_g[l], w_o_attn=w_o_attn[l],
                 lam_re_f=lam_re_f[l], lam_im_f=lam_im_f[l], log_dt_f=log_dt_f[l],
                 c_re_f=c_re_f[l], c_im_f=c_im_f[l],
                 lam_re_b=lam_re_b[l], lam_im_b=lam_im_b[l], log_dt_b=log_dt_b[l],
                 c_re_b=c_re_b[l], c_im_b=c_im_b[l],
                 b_re=b_re[l], b_im=b_im[l], d_skip=d_skip[l], w_glu=w_glu[l], w_out=w_out[l],
                 w_up=w_up[l], conv_w=conv_w[l], conv_b=conv_b[l], w_down=w_down[l])
        x, ctx = hybrid_layer(x, ctx, c, c_ctx, p, cos, sin, update_ctx=(l < DEPTH - 1))
    return x
```

```python
import functools
import math

import jax
import jax.numpy as jnp
from jax import lax
from jax.experimental import pallas as pl
from jax.experimental.pallas import tpu as pltpu

D_MODEL = 1024
GRID_W = 64
N_HEADS = 8
QK_NOPE = 64
QK_ROPE = 32
QK_DIM = QK_NOPE + QK_ROPE
V_DIM = 64
Q_LORA = 384
KV_LORA = 256
ROPE_THETA = 10000.0
SSM_WIDTH = 512
SSM_GROUP = 16
SSM_GROUPS = SSM_WIDTH // SSM_GROUP
SSM_STATE = 64
FFN_HIDDEN = 2816
EPS = 1e-6

LANES = 128
HEAD_PAD = LANES
SSM_CHUNK = 16
SSM_ROWS = 8
FFN_CHUNK = 256
VMEM_LIMIT = 56 * 1024 * 1024

_HI = lax.Precision.HIGHEST
_F32 = jnp.float32
_BF16 = jnp.bfloat16


def _params(sem):
    return pltpu.CompilerParams(dimension_semantics=sem, vmem_limit_bytes=VMEM_LIMIT)


def _const_spec(shape):
    nd = len(shape)
    return pl.BlockSpec(shape, lambda *_: (0,) * nd)


def _rms(v, width):
    return v * lax.rsqrt(jnp.sum(v * v, axis=-1, keepdims=True) * (1.0 / width) + EPS)


def _mod_kernel(c_ref, w_ref, b_ref, o_ref):
    c = c_ref[...]
    s = c * jax.nn.sigmoid(c)
    o_ref[...] = jnp.dot(s, w_ref[...], precision=_HI, preferred_element_type=_F32) + b_ref[...]


def _mod_call(cc, w_mod, b_mod):
    rows, d = cc.shape
    n = w_mod.shape[1]
    tn = 1024
    return pl.pallas_call(
        _mod_kernel,
        grid=(n // tn,),
        in_specs=[
            pl.BlockSpec((rows, d), lambda j: (0, 0)),
            pl.BlockSpec((d, tn), lambda j: (0, j)),
            pl.BlockSpec((1, tn), lambda j: (0, j)),
        ],
        out_specs=pl.BlockSpec((rows, tn), lambda j: (0, j)),
        out_shape=jax.ShapeDtypeStruct((rows, n), _F32),
        compiler_params=_params(("arbitrary",)),
        name="mod",
    )(cc, w_mod, b_mod.reshape(1, n))


_C_CQ = 0
_C_CKV = _C_CQ + Q_LORA
_C_U = _C_CKV + KV_LORA
_C_GL = _C_U + SSM_WIDTH
_C_KR = _C_GL + 2 * D_MODEL
_IN_W = _C_KR + LANES


def _rope(t, ctab, s1tab, s2tab):
    up = pltpu.roll(t, LANES - QK_ROPE // 2, 1)
    dn = pltpu.roll(t, QK_ROPE // 2, 1)
    return t * ctab + up * s1tab + dn * s2tab


def _inproj_kernel(x_ref, mod_ref, g1_ref, win_ref, qag_ref, wuq_ref, kvag_ref, wukv_ref,
                   qng_ref, kng_ref, ctab_ref, s1_ref, s2_ref,
                   q_ref, kt_ref, v_ref, u_ref, sg_ref):
    x = x_ref[0]
    mod = mod_ref[0]
    sh1 = mod[0:1, :]
    sc1 = mod[1:2, :]
    h = _rms(x, D_MODEL) * g1_ref[...]
    h = h * (1.0 + sc1) + sh1
    proj = jnp.dot(h.astype(_BF16), win_ref[...], preferred_element_type=_F32)

    u_ref[0] = proj[:, _C_U:_C_GL]
    sg_ref[0] = jax.nn.sigmoid(proj[:, _C_GL:_C_KR]).astype(_BF16)

    ctab = ctab_ref[...]
    s1tab = s1_ref[...]
    s2tab = s2_ref[...]

    cq = _rms(proj[:, _C_CQ:_C_CKV], Q_LORA) * qag_ref[...]
    qall = jnp.dot(cq.astype(_BF16), wuq_ref[...], preferred_element_type=_F32)
    ckv = _rms(proj[:, _C_CKV:_C_U], KV_LORA) * kvag_ref[...]
    kvall = jnp.dot(ckv.astype(_BF16), wukv_ref[...], preferred_element_type=_F32)
    krslab = proj[:, _C_KR:_IN_W]
    qg = qng_ref[...] * (QK_DIM ** -0.5)
    kg = kng_ref[...]
    for hd in range(N_HEADS):
        lo = hd * HEAD_PAD
        qh = _rms(qall[:, lo:lo + HEAD_PAD], QK_DIM) * qg
        q_ref[0, hd] = _rope(qh, ctab, s1tab, s2tab).astype(_BF16)
        kh = _rms(kvall[:, lo:lo + HEAD_PAD] + krslab, QK_DIM) * kg
        kh = _rope(kh, ctab, s1tab, s2tab)
        kt_ref[0, hd] = kh.T.astype(_BF16)
    v_ref[0] = kvall[:, N_HEADS * HEAD_PAD:].astype(_BF16)


def _inproj_call(x, mod6, g1, win, qag, wuq, kvag, wukv, qng, kng, ctab, s1tab, s2tab, tm):
    b, s, d = x.shape
    grid = (b, s // tm)
    tok = lambda w: pl.BlockSpec((1, tm, w), lambda i, j: (i, j, 0))
    tab = pl.BlockSpec((tm, LANES), lambda i, j: (j, 0))
    return pl.pallas_call(
        _inproj_kernel,
        grid=grid,
        in_specs=[
            tok(d),
            pl.BlockSpec((1, 6, d), lambda i, j: (i, 0, 0)),
            _const_spec(g1.shape), _const_spec(win.shape), _const_spec(qag.shape),
            _const_spec(wuq.shape), _const_spec(kvag.shape), _const_spec(wukv.shape),
            _const_spec(qng.shape), _const_spec(kng.shape),
            tab, tab, tab,
        ],
        out_specs=[
            pl.BlockSpec((1, N_HEADS, tm, HEAD_PAD), lambda i, j: (i, 0, j, 0)),
            pl.BlockSpec((1, N_HEADS, HEAD_PAD, tm), lambda i, j: (i, 0, 0, j)),
            tok(N_HEADS * V_DIM), tok(SSM_WIDTH), tok(2 * D_MODEL),
        ],
        out_shape=[
            jax.ShapeDtypeStruct((b, N_HEADS, s, HEAD_PAD), _BF16),
            jax.ShapeDtypeStruct((b, N_HEADS, HEAD_PAD, s), _BF16),
            jax.ShapeDtypeStruct((b, s, N_HEADS * V_DIM), _BF16),
            jax.ShapeDtypeStruct((b, s, SSM_WIDTH), _F32),
            jax.ShapeDtypeStruct((b, s, 2 * D_MODEL), _BF16),
        ],
        compiler_params=_params(("parallel", "parallel")),
        name="inproj",
    )(x, mod6, g1, win, qag, wuq, kvag, wukv, qng, kng, ctab, s1tab, s2tab)


def _attn_kernel(q_ref, kt_ref, v_ref, o_ref, *, tk, n_kt):
    tq = q_ref.shape[2]
    lane = lax.broadcasted_iota(jnp.int32, (tq, LANES), 1)
    outs = []
    for j in range(2):
        q = q_ref[0, j]

        def body(t, carry):
            m, l, acc = carry
            off = pl.multiple_of(t * tk, tk)
            s = jnp.dot(q, kt_ref[0, j, :, pl.ds(off, tk)], preferred_element_type=_F32)
            m_new = jnp.maximum(m, jnp.max(s, axis=-1, keepdims=True))
            alpha = jnp.exp(m - m_new)
            p = jnp.exp(s - m_new)
            l = alpha * l + jnp.sum(p, axis=-1, keepdims=True)
            pv = jnp.dot(p.astype(_BF16), v_ref[0, pl.ds(off, tk), :], preferred_element_type=_F32)
            return m_new, l, alpha * acc + pv

        m0 = jnp.full((tq, 1), -jnp.inf, _F32)
        l0 = jnp.zeros((tq, 1), _F32)
        a0 = jnp.zeros((tq, LANES), _F32)
        m, l, acc = lax.fori_loop(0, n_kt, body, (m0, l0, a0))
        outs.append(acc / l)
    o_ref[0] = jnp.where(lane < V_DIM, outs[0], outs[1]).astype(_BF16)


def _attn_call(q, kt, v, tq, tk):
    b, h, s, _ = q.shape
    nk = kt.shape[3]
    n_kt = nk // tk
    kern = functools.partial(_attn_kernel, tk=tk, n_kt=n_kt)
    return pl.pallas_call(
        kern,
        grid=(b, h // 2, s // tq),
        in_specs=[
            pl.BlockSpec((1, 2, tq, HEAD_PAD), lambda i, p, j: (i, p, j, 0)),
            pl.BlockSpec((1, 2, HEAD_PAD, nk), lambda i, p, j: (i, p, 0, 0)),
            pl.BlockSpec((1, nk, LANES), lambda i, p, j: (i, 0, p)),
        ],
        out_specs=pl.BlockSpec((1, tq, LANES), lambda i, p, j: (i, j, p)),
        out_shape=jax.ShapeDtypeStruct((b, s, h * V_DIM), _BF16),
        compiler_params=_params(("parallel", "parallel", "arbitrary")),
        name="attn",
    )(q, kt, v)


def _cmul(ar, ai, br, bi):
    return ar * br - ai * bi, ar * bi + ai * br


def _ssmprep_kernel(lre_ref, lim_ref, ldt_ref, btre_ref, btim_ref, cre_ref, cim_ref, dsk_ref,
                    wtre_ref, wtim_ref, cpre_ref, cpim_ref, k_ref, are_ref, aim_ref, *, reverse):
    lre = lre_ref[0]
    lim = lim_ref[0]
    dt = jnp.exp(ldt_ref[0])
    mag = jnp.exp(lre * dt)
    ar = mag * jnp.cos(lim * dt)
    ai = mag * jnp.sin(lim * dt)
    nr = ar - 1.0
    den = lre * lre + lim * lim
    cfr = (nr * lre + ai * lim) / den
    cfi = (ai * lre - nr * lim) / den
    bbr, bbi = _cmul(btre_ref[0], btim_ref[0], cfr, cfi)
    cre = cre_ref[0]
    cim = cim_ref[0]
    pr = [jnp.ones_like(ar)]
    pi = [jnp.zeros_like(ar)]
    for _ in range(SSM_CHUNK):
        r, i = _cmul(pr[-1], pi[-1], ar, ai)
        pr.append(r)
        pi.append(i)
    are_ref[0] = pr[SSM_CHUNK]
    aim_ref[0] = pi[SSM_CHUNK]
    wr, wi, cr, ci = [], [], [], []
    for k in range(SSM_CHUNK + 1):
        r, i = _cmul(cre, cim, pr[k], pi[k])
        cr.append(r)
        ci.append(i)
    for s in range(SSM_CHUNK):
        k = s if reverse else SSM_CHUNK - 1 - s
        r, i = _cmul(bbr, bbi, pr[k], pi[k])
        wr.append(r)
        wi.append(i)
    wtre_ref[0] = jnp.concatenate(wr, axis=0)
    wtim_ref[0] = jnp.concatenate(wi, axis=0)
    cpr = jnp.concatenate(cr, axis=0)
    cpi = jnp.concatenate(ci, axis=0)
    cpre_ref[0] = cpr
    cpim_ref[0] = cpi
    nk = SSM_CHUNK * SSM_GROUP
    dn = (((1,), (1,)), ((), ()))
    kk = (lax.dot_general(cpr[:nk], bbr, dn, precision=_HI, preferred_element_type=_F32)
          - lax.dot_general(cpi[:nk], bbi, dn, precision=_HI, preferred_element_type=_F32))
    if not reverse:
        row = lax.broadcasted_iota(jnp.int32, (nk, SSM_GROUP), 0)
        col = lax.broadcasted_iota(jnp.int32, (nk, SSM_GROUP), 1)
        kk = kk + jnp.where(row == col, dsk_ref[0], 0.0)
    k_ref[0] = kk


def _ssmprep_call(lre, lim, ldt, btre, btim, cre, cim, dsk, reverse):
    g, n = lre.shape
    c = SSM_GROUP
    l = SSM_CHUNK
    per_g = lambda *shape: pl.BlockSpec((1,) + shape, lambda i: (i,) + (0,) * len(shape))
    out = lambda *shape: jax.ShapeDtypeStruct((g,) + shape, _F32)
    return pl.pallas_call(
        functools.partial(_ssmprep_kernel, reverse=reverse),
        grid=(g,),
        in_specs=[per_g(1, n), per_g(1, n), per_g(1, n), per_g(c, n), per_g(c, n),
                  per_g(c, n), per_g(c, n), per_g(1, c)],
        out_specs=[per_g(l * c, n), per_g(l * c, n), per_g((l + 1) * c, n), per_g((l + 1) * c, n),
                   per_g(l * c, c), per_g(1, n), per_g(1, n)],
        out_shape=[out(l * c, n), out(l * c, n), out((l + 1) * c, n), out((l + 1) * c, n),
                   out(l * c, c), out(1, n), out(1, n)],
        compiler_params=_params(("arbitrary",)),
        name="ssmprep_b" if reverse else "ssmprep_f",
    )(lre.reshape(g, 1, n), lim.reshape(g, 1, n), jnp.broadcast_to(ldt[:, None, None], (g, 1, n)),
      btre, btim, cre, cim, dsk.reshape(g, 1, c))


def _ssm_tables(lam_re_f, lam_im_f, log_dt_f, c_re_f, c_im_f,
                lam_re_b, lam_im_b, log_dt_b, c_re_b, c_im_b, b_re, b_im, d_skip):
    g, n, c, l = SSM_GROUPS, SSM_STATE, SSM_GROUP, SSM_CHUNK
    btre = b_re.transpose(0, 2, 1)
    btim = b_im.transpose(0, 2, 1)
    dsk = d_skip.reshape(g, c)
    wfr, wfi, cfr, cfi, kf, afr, afi = _ssmprep_call(
        lam_re_f, lam_im_f, log_dt_f, btre, btim, c_re_f, c_im_f, dsk, False)
    wbr, wbi, cbr, cbi, kb, abr, abi = _ssmprep_call(
        lam_re_b, lam_im_b, log_dt_b, btre, btim, c_re_b, c_im_b, dsk, True)
    wt = jnp.concatenate([wfr, wbr, wfi, wbi], axis=-1).astype(_BF16)

    def readout(cp_f, cp_b):
        f = cp_f.reshape(g, l + 1, c, n)[:, 1:]
        b = cp_b.reshape(g, l + 1, c, n)[:, 1:][:, ::-1]
        f = f.reshape(g, l * c, n).transpose(0, 2, 1)
        b = b.reshape(g, l * c, n).transpose(0, 2, 1)
        return jnp.concatenate([f, b], axis=1)

    r = jnp.concatenate([readout(cfr, cbr), -readout(cfi, cbi)], axis=1).astype(_BF16)
    t_idx = jnp.arange(l)
    lag = t_idx[None, :] - t_idx[:, None]
    kf4 = kf.reshape(g, l, c, c)
    kb4 = kb.reshape(g, l, c, c)
    tf = jnp.where((lag >= 0)[None, :, :, None, None], kf4[:, jnp.clip(lag, 0, l - 1)], 0.0)
    tb = jnp.where((lag <= 0)[None, :, :, None, None], kb4[:, jnp.clip(-lag, 0, l - 1)], 0.0)
    tt = (tf + tb).transpose(0, 1, 4, 2, 3).reshape(g, l * c, l * c).astype(_BF16)
    a = jnp.stack([jnp.concatenate([afr, abr], axis=-1), jnp.concatenate([afi, abi], axis=-1)],
                  axis=1).reshape(g, 2, 2 * n)
    return wt, tt, r, a


def _ssm_kernel(u_ref, wt_ref, tt_ref, r_ref, a_ref, init_ref, *rest, nc, with_y):
    if with_y:
        y_ref, fin_ref, ere, eim, xfre, xfim, xbre, xbim = rest
    else:
        fin_ref, ere, eim, xfre, xfim, xbre, xbim = rest
    u = u_ref[0]
    e = jnp.dot(u, wt_ref[0], preferred_element_type=_F32)
    e = e.reshape(nc, SSM_ROWS, 2 * LANES)
    ere[...] = e[:, :, :LANES]
    eim[...] = e[:, :, LANES:]
    a = a_ref[0]
    a_re = jnp.broadcast_to(a[0:1, :], (SSM_ROWS, LANES))
    a_im = jnp.broadcast_to(a[1:2, :], (SSM_ROWS, LANES))
    fwd = lax.broadcasted_iota(jnp.int32, (SSM_ROWS, LANES), 1) < SSM_STATE

    def step(c, carry):
        x_re, x_im = carry
        cb = nc - 1 - c
        xfre[c] = x_re
        xfim[c] = x_im
        xbre[cb] = x_re
        xbim[cb] = x_im
        e_re = jnp.where(fwd, ere[c], ere[cb])
        e_im = jnp.where(fwd, eim[c], eim[cb])
        n_re = a_re * x_re - a_im * x_im + e_re
        n_im = a_re * x_im + a_im * x_re + e_im
        return n_re, n_im

    x_re, x_im = lax.fori_loop(0, nc, step, (init_ref[0, 0], init_ref[0, 1]))
    fin_ref[0, 0] = x_re
    fin_ref[0, 1] = x_im
    if with_y:
        fwd3 = lax.broadcasted_iota(jnp.int32, (nc, SSM_ROWS, LANES), 2) < SSM_STATE
        xin_re = jnp.where(fwd3, xfre[...], xbre[...]).reshape(nc * SSM_ROWS, LANES)
        xin_im = jnp.where(fwd3, xfim[...], xbim[...]).reshape(nc * SSM_ROWS, LANES)
        xin = jnp.concatenate([xin_re, xin_im], axis=-1).astype(_BF16)
        y = jnp.dot(u, tt_ref[0], preferred_element_type=_F32)
        y = y + jnp.dot(xin, r_ref[0], preferred_element_type=_F32)
        y_ref[0] = y


def _ssm_call(u_g, wt, tt, r, a, init, with_y):
    g, nr, w = u_g.shape
    nc = nr // SSM_ROWS
    per_g = lambda *shape: pl.BlockSpec((1,) + shape, lambda i: (i,) + (0,) * len(shape))
    fin_shape = jax.ShapeDtypeStruct((g, 2, SSM_ROWS, LANES), _F32)
    out_specs = [per_g(2, SSM_ROWS, LANES)]
    out_shape = [fin_shape]
    if with_y:
        out_specs = [per_g(nr, w)] + out_specs
        out_shape = [jax.ShapeDtypeStruct((g, nr, w), _F32)] + out_shape
    scratch = [pltpu.VMEM((nc, SSM_ROWS, LANES), _F32) for _ in range(6)]
    return pl.pallas_call(
        functools.partial(_ssm_kernel, nc=nc, with_y=with_y),
        grid=(g,),
        in_specs=[per_g(nr, w), per_g(w, w), per_g(w, w), per_g(w, w), per_g(2, LANES),
                  per_g(2, SSM_ROWS, LANES)],
        out_specs=out_specs,
        out_shape=out_shape,
        scratch_shapes=scratch,
        compiler_params=_params(("arbitrary",)),
        name="ssm_lat" if with_y else "ssm_ctx",
    )(u_g, wt, tt, r, a, init)


def _to_groups(u):
    b, s, _ = u.shape
    nc = s // SSM_CHUNK
    ug = u.astype(_BF16).reshape(b, nc, SSM_CHUNK, SSM_GROUPS, SSM_GROUP).transpose(3, 1, 0, 2, 4)
    ug = ug.reshape(SSM_GROUPS, nc, b, SSM_CHUNK * SSM_GROUP)
    ug = jnp.pad(ug, ((0, 0), (0, 0), (0, SSM_ROWS - b), (0, 0)))
    return ug.reshape(SSM_GROUPS, nc * SSM_ROWS, SSM_CHUNK * SSM_GROUP)


def _from_groups(y, b):
    g, nr, _ = y.shape
    nc = nr // SSM_ROWS
    y = y.reshape(g, nc, SSM_ROWS, SSM_CHUNK, SSM_GROUP)[:, :, :b]
    return y.transpose(2, 1, 3, 0, 4).reshape(b, nc * SSM_CHUNK, SSM_WIDTH)


def _mix_kernel(x_ref, mod_ref, o_ref, y_ref, sg_ref, wo_ref, wglu_ref, wout_ref, x1_ref):
    a = jnp.dot(o_ref[0], wo_ref[...], preferred_element_type=_F32)
    yg = jax.nn.gelu(y_ref[0])
    glu = jnp.dot(yg.astype(_BF16), wglu_ref[...], preferred_element_type=_F32)
    s = glu[:, :D_MODEL] * jax.nn.sigmoid(glu[:, D_MODEL:])
    sg = sg_ref[0].astype(_F32)
    merged = sg[:, :D_MODEL] * a + sg[:, D_MODEL:] * s
    out = jnp.dot(merged.astype(_BF16), wout_ref[...], preferred_element_type=_F32)
    g1 = mod_ref[0][2:3, :]
    x1_ref[0] = x_ref[0] + g1 * out


def _mix_call(x, mod6, o, y, sg, wo, wglu, wout, tm):
    b, s, d = x.shape
    tok = lambda w: pl.BlockSpec((1, tm, w), lambda i, j: (i, j, 0))
    return pl.pallas_call(
        _mix_kernel,
        grid=(b, s // tm),
        in_specs=[tok(d), pl.BlockSpec((1, 6, d), lambda i, j: (i, 0, 0)),
                  tok(o.shape[2]), tok(y.shape[2]), tok(sg.shape[2]),
                  _const_spec(wo.shape), _const_spec(wglu.shape), _const_spec(wout.shape)],
        out_specs=tok(d),
        out_shape=jax.ShapeDtypeStruct((b, s, d), _F32),
        compiler_params=_params(("parallel", "parallel")),
        name="mix",
    )(x, mod6, o, y, sg, wo, wglu, wout)


def _ffn_kernel(x_ref, prev_ref, next_ref, mod_ref, g2_ref, wup_ref, cw_ref, cb_ref, wdn_ref,
                out_ref, acc_ref, *, n_chunks):
    j = pl.program_id(1)
    nj = pl.num_programs(1)
    tm = x_ref.shape[1]
    mod = mod_ref[0]
    sh2 = mod[3:4, :]
    sc2 = mod[4:5, :]
    g2 = mod[5:6, :]
    gain = g2_ref[...]

    def prenorm(v):
        return ((_rms(v, D_MODEL) * gain) * (1.0 + sc2) + sh2).astype(_BF16)

    x = x_ref[0]
    h = prenorm(x)
    hh = prenorm(jnp.concatenate([prev_ref[0], next_ref[0]], axis=0))
    has_prev = (j > 0).astype(_F32)
    has_next = (j < nj - 1).astype(_F32)
    row = lax.broadcasted_iota(jnp.int32, (tm, 2 * FFN_CHUNK), 0)
    acc_ref[...] = jnp.zeros_like(acc_ref)

    def body(ci, carry):
        co = pl.multiple_of(ci * 2 * FFN_CHUNK, 2 * FFN_CHUNK)
        ro = pl.multiple_of(ci * FFN_CHUNK, FFN_CHUNK)
        w = wup_ref[:, pl.ds(co, 2 * FFN_CHUNK)]
        p = jnp.dot(h, w, preferred_element_type=_F32)
        ph = jnp.dot(hh, w, preferred_element_type=_F32)
        before = jnp.where(row == 0, ph[7:8, :] * has_prev, pltpu.roll(p, 1, 0))
        after = jnp.where(row == tm - 1, ph[8:9, :] * has_next, pltpu.roll(p, tm - 1, 0))
        cw = cw_ref[:, pl.ds(co, 2 * FFN_CHUNK)]
        uc = before * cw[0:1, :] + p * cw[1:2, :] + after * cw[2:3, :] + cb_ref[:, pl.ds(co, 2 * FFN_CHUNK)]
        val = uc[:, :FFN_CHUNK]
        gate = uc[:, FFN_CHUNK:]
        act = (gate * jax.nn.sigmoid(gate) * val).astype(_BF16)
        acc_ref[...] += jnp.dot(act, wdn_ref[pl.ds(ro, FFN_CHUNK), :], preferred_element_type=_F32)
        return carry

    lax.fori_loop(0, n_chunks, body, 0)
    out_ref[0] = x + g2 * acc_ref[...]


def _ffn_call(x1, mod6, g2, wup, cw, cb, wdn, tm):
    b, s, d = x1.shape
    n_chunks = wdn.shape[0] // FFN_CHUNK
    hb = tm // 8
    last = s // 8 - 1
    return pl.pallas_call(
        functools.partial(_ffn_kernel, n_chunks=n_chunks),
        grid=(b, s // tm),
        in_specs=[
            pl.BlockSpec((1, tm, d), lambda i, j: (i, j, 0)),
            pl.BlockSpec((1, 8, d), lambda i, j: (i, jnp.maximum(j * hb - 1, 0), 0)),
            pl.BlockSpec((1, 8, d), lambda i, j: (i, jnp.minimum((j + 1) * hb, last), 0)),
            pl.BlockSpec((1, 6, d), lambda i, j: (i, 0, 0)),
            _const_spec(g2.shape), _const_spec(wup.shape), _const_spec(cw.shape),
            _const_spec(cb.shape), _const_spec(wdn.shape),
        ],
        out_specs=pl.BlockSpec((1, tm, d), lambda i, j: (i, j, 0)),
        out_shape=jax.ShapeDtypeStruct((b, s, d), _F32),
        scratch_shapes=[pltpu.VMEM((tm, d), _F32)],
        compiler_params=_params(("parallel", "arbitrary")),
        name="ffn",
    )(x1, x1, x1, mod6, g2, wup, cw, cb, wdn)


def _rope_tables(s):
    rows = s // GRID_W
    row = jnp.repeat(jnp.arange(rows), GRID_W)
    col = jnp.tile(jnp.arange(GRID_W), rows)
    pairs = QK_ROPE // 4
    freqs = ROPE_THETA ** (-jnp.arange(pairs, dtype=_F32) / pairs)
    ang = jnp.concatenate([row[:, None] * freqs, col[:, None] * freqs], axis=-1)
    cos, sin = jnp.cos(ang), jnp.sin(ang)
    half = QK_ROPE // 2
    z = lambda w: jnp.zeros((s, w), _F32)
    ctab = jnp.concatenate([jnp.ones((s, QK_NOPE), _F32), cos, cos, z(HEAD_PAD - QK_DIM)], axis=-1)
    s1 = jnp.concatenate([z(QK_NOPE), -sin, z(HEAD_PAD - QK_NOPE - half)], axis=-1)
    s2 = jnp.concatenate([z(QK_NOPE + half), sin, z(HEAD_PAD - QK_DIM)], axis=-1)
    return ctab, s1, s2


def _identity_tables(s):
    ctab = jnp.concatenate([jnp.ones((s, QK_DIM), _F32), jnp.zeros((s, HEAD_PAD - QK_DIM), _F32)], axis=-1)
    z = jnp.zeros((s, HEAD_PAD), _F32)
    return ctab, z, z


def _pick_tile(n, pref):
    t = min(n, pref)
    while n % t:
        t //= 2
    return t


def kernel(x, c, ctx, c_ctx, w_mod, b_mod, norm1_g, norm2_g, w_in, q_a_g, w_uq, kv_a_g, w_ukv, q_norm_g, k_norm_g, w_o_attn, lam_re_f, lam_im_f, log_dt_f, c_re_f, c_im_f, lam_re_b, lam_im_b, log_dt_b, c_re_b, c_im_b, b_re, b_im, d_skip, w_glu, w_out, w_up, conv_w, conv_b, w_down):
    b, s, d = x.shape
    n_ctx = ctx.shape[1]
    depth = w_mod.shape[0]
    assert depth == 1, "context update between layers is not implemented"
    l = 0

    wi = w_in[l]
    o1, o2, o3, o4 = Q_LORA, Q_LORA + KV_LORA, Q_LORA + KV_LORA + QK_ROPE, Q_LORA + KV_LORA + QK_ROPE + SSM_WIDTH
    zc = lambda w: jnp.zeros((d, w), wi.dtype)
    win = jnp.concatenate([wi[:, :o1], wi[:, o1:o2], wi[:, o3:o4], wi[:, o4:],
                           zc(QK_NOPE), wi[:, o2:o3], zc(HEAD_PAD - QK_DIM)], axis=-1).astype(_BF16)
    wuq = jnp.pad(w_uq[l].reshape(Q_LORA, N_HEADS, QK_DIM), ((0, 0), (0, 0), (0, HEAD_PAD - QK_DIM)))
    wuq = wuq.reshape(Q_LORA, N_HEADS * HEAD_PAD).astype(_BF16)
    wkv = w_ukv[l].reshape(KV_LORA, N_HEADS, QK_NOPE + V_DIM)
    wk = jnp.pad(wkv[:, :, :QK_NOPE], ((0, 0), (0, 0), (0, HEAD_PAD - QK_NOPE))).reshape(KV_LORA, N_HEADS * HEAD_PAD)
    wv = wkv[:, :, QK_NOPE:].reshape(KV_LORA, N_HEADS * V_DIM)
    wukv = jnp.concatenate([wk, wv], axis=-1).astype(_BF16)
    padg = lambda g: jnp.pad(g, (0, HEAD_PAD - QK_DIM)).reshape(1, HEAD_PAD)
    qng, kng = padg(q_norm_g[l]), padg(k_norm_g[l])
    g1 = norm1_g[l].reshape(1, d)
    g2 = norm2_g[l].reshape(1, d)
    qag = q_a_g[l].reshape(1, Q_LORA)
    kvag = kv_a_g[l].reshape(1, KV_LORA)
    nch = FFN_HIDDEN // FFN_CHUNK
    pair = lambda w: jnp.concatenate(
        [w[..., :FFN_HIDDEN].reshape(w.shape[:-1] + (nch, 1, FFN_CHUNK)),
         w[..., FFN_HIDDEN:].reshape(w.shape[:-1] + (nch, 1, FFN_CHUNK))], axis=-2
    ).reshape(w.shape[:-1] + (2 * FFN_HIDDEN,))
    wup = pair(w_up[l]).astype(_BF16)
    cw = pair(conv_w[l])
    cb = pair(conv_b[l].reshape(1, 2 * FFN_HIDDEN))
    wdn = w_down[l].astype(_BF16)
    wo = w_o_attn[l].astype(_BF16)
    wglu = w_glu[l].astype(_BF16)
    wout = w_out[l].astype(_BF16)

    cc = jnp.concatenate([c, c_ctx[None, :], jnp.zeros((8 - b - 1, d), c.dtype)], axis=0)
    mod = _mod_call(cc, w_mod[l], b_mod[l])
    mod_lat = mod[:b].reshape(b, 6, d)
    mod_ctx = jnp.broadcast_to(mod[b].reshape(1, 6, d), (b, 6, d))

    tm = _pick_tile(s, 512)
    tmc = _pick_tile(n_ctx, 512)
    shared = (g1, win, qag, wuq, kvag, wukv, qng, kng)
    q, kt, v, u, sg = _inproj_call(x, mod_lat, *shared, *_rope_tables(s), tm)
    _, kt_c, v_c, u_c, _ = _inproj_call(ctx, mod_ctx, *shared, *_identity_tables(n_ctx), tmc)

    wt, tt, r, a = _ssm_tables(lam_re_f[l], lam_im_f[l], log_dt_f[l], c_re_f[l], c_im_f[l],
                               lam_re_b[l], lam_im_b[l], log_dt_b[l], c_re_b[l], c_im_b[l],
                               b_re[l], b_im[l], d_skip[l])
    zero_init = jnp.zeros((SSM_GROUPS, 2, SSM_ROWS, LANES), _F32)
    (fin_c,) = _ssm_call(_to_groups(u_c), wt, tt, r, a, zero_init, False)
    y_g, _ = _ssm_call(_to_groups(u), wt, tt, r, a, fin_c, True)
    y = _from_groups(y_g, b)

    kt_all = jnp.concatenate([kt, kt_c], axis=-1)
    v_all = jnp.concatenate([v, v_c], axis=1)
    tq = _pick_tile(s, 512)
    tk = _pick_tile(s + n_ctx, 256)
    o = _attn_call(q, kt_all, v_all, tq, tk)

    x1 = _mix_call(x, mod_lat, o, y, sg, wo, wglu, wout, tm)
    return _ffn_call(x1, mod_lat, g2, wup, cw, cb, wdn, tm)
```

```python
import functools
import math

import jax
import jax.numpy as jnp
from jax import lax
from jax.experimental import pallas as pl
from jax.experimental.pallas import tpu as pltpu

D_MODEL = 1024
GRID_W = 64
N_HEADS = 8
QK_NOPE = 64
QK_ROPE = 32
QK_DIM = QK_NOPE + QK_ROPE
V_DIM = 64
Q_LORA = 384
KV_LORA = 256
ROPE_THETA = 10000.0
SSM_WIDTH = 512
SSM_GROUP = 16
SSM_GROUPS = SSM_WIDTH // SSM_GROUP
SSM_STATE = 64
FFN_HIDDEN = 2816
EPS = 1e-6

LANES = 128
HEAD_PAD = LANES
ONES_ROWS = 16
ATTN_UNROLL = 4
SSM_CHUNK = 16
SSM_ROWS = 8
FFN_CHUNK = 256
VMEM_LIMIT = 56 * 1024 * 1024

_HI = lax.Precision.HIGHEST
_F32 = jnp.float32
_BF16 = jnp.bfloat16


def _params(sem):
    return pltpu.CompilerParams(dimension_semantics=sem, vmem_limit_bytes=VMEM_LIMIT)


def _const_spec(shape):
    nd = len(shape)
    return pl.BlockSpec(shape, lambda *_: (0,) * nd)


def _rms(v, width):
    return v * lax.rsqrt(jnp.sum(v * v, axis=-1, keepdims=True) * (1.0 / width) + EPS)


def _mod_kernel(c_ref, w_ref, b_ref, o_ref):
    c = c_ref[...]
    s = c * jax.nn.sigmoid(c)
    o_ref[...] = jnp.dot(s, w_ref[...], precision=_HI, preferred_element_type=_F32) + b_ref[...]


def _mod_call(cc, w_mod, b_mod):
    rows, d = cc.shape
    n = w_mod.shape[1]
    tn = 1024
    return pl.pallas_call(
        _mod_kernel,
        grid=(n // tn,),
        in_specs=[
            pl.BlockSpec((rows, d), lambda j: (0, 0)),
            pl.BlockSpec((d, tn), lambda j: (0, j)),
            pl.BlockSpec((1, tn), lambda j: (0, j)),
        ],
        out_specs=pl.BlockSpec((rows, tn), lambda j: (0, j)),
        out_shape=jax.ShapeDtypeStruct((rows, n), _F32),
        compiler_params=_params(("arbitrary",)),
        name="mod",
    )(cc, w_mod, b_mod.reshape(1, n))


_C_CQ = 0
_C_CKV = _C_CQ + Q_LORA
_C_U = _C_CKV + KV_LORA
_C_GL = _C_U + SSM_WIDTH
_C_KR = _C_GL + 2 * D_MODEL
_IN_W = _C_KR + LANES


def _rope(t, ctab, s1tab, s2tab):
    up = pltpu.roll(t, LANES - QK_ROPE // 2, 1)
    dn = pltpu.roll(t, QK_ROPE // 2, 1)
    return t * ctab + up * s1tab + dn * s2tab


def _inproj_kernel(x_ref, mod_ref, g1_ref, win_ref, qag_ref, wuq_ref, kvag_ref, wukv_ref,
                   qng_ref, kng_ref, ctab_ref, s1_ref, s2_ref,
                   qt_ref, k_ref, vt_ref, u_ref, sg_ref):
    x = x_ref[0]
    mod = mod_ref[0]
    sh1 = mod[0:1, :]
    sc1 = mod[1:2, :]
    h = _rms(x, D_MODEL) * g1_ref[...]
    h = h * (1.0 + sc1) + sh1
    proj = jnp.dot(h.astype(_BF16), win_ref[...], preferred_element_type=_F32)

    u_ref[0] = proj[:, _C_U:_C_GL]
    sg_ref[0] = jax.nn.sigmoid(proj[:, _C_GL:_C_KR]).astype(_BF16)

    ctab = ctab_ref[...]
    s1tab = s1_ref[...]
    s2tab = s2_ref[...]

    cq = _rms(proj[:, _C_CQ:_C_CKV], Q_LORA) * qag_ref[...]
    qall = jnp.dot(cq.astype(_BF16), wuq_ref[...], preferred_element_type=_F32)
    ckv = _rms(proj[:, _C_CKV:_C_U], KV_LORA) * kvag_ref[...]
    kvall = jnp.dot(ckv.astype(_BF16), wukv_ref[...], preferred_element_type=_F32)
    krslab = proj[:, _C_KR:_IN_W]
    qg = qng_ref[...] * (QK_DIM ** -0.5 * math.log2(math.e))
    kg = kng_ref[...]
    for hd in range(N_HEADS):
        lo = hd * HEAD_PAD
        qh = _rms(qall[:, lo:lo + HEAD_PAD], QK_DIM) * qg
        qt_ref[0, hd] = _rope(qh, ctab, s1tab, s2tab).T.astype(_BF16)
        kh = _rms(kvall[:, lo:lo + HEAD_PAD] + krslab, QK_DIM) * kg
        k_ref[0, hd] = _rope(kh, ctab, s1tab, s2tab).astype(_BF16)
    for hp in range(N_HEADS // 2):
        lo = N_HEADS * HEAD_PAD + hp * LANES
        vt = kvall[:, lo:lo + LANES].T.astype(_BF16)
        vt_ref[0, 2 * hp] = vt[:V_DIM]
        vt_ref[0, 2 * hp + 1] = vt[V_DIM:]


def _inproj_call(x, mod6, g1, win, qag, wuq, kvag, wukv, qng, kng, ctab, s1tab, s2tab, tm):
    b, s, d = x.shape
    grid = (b, s // tm)
    tok = lambda w: pl.BlockSpec((1, tm, w), lambda i, j: (i, j, 0))
    tab = pl.BlockSpec((tm, LANES), lambda i, j: (j, 0))
    return pl.pallas_call(
        _inproj_kernel,
        grid=grid,
        in_specs=[
            tok(d),
            pl.BlockSpec((1, 6, d), lambda i, j: (i, 0, 0)),
            _const_spec(g1.shape), _const_spec(win.shape), _const_spec(qag.shape),
            _const_spec(wuq.shape), _const_spec(kvag.shape), _const_spec(wukv.shape),
            _const_spec(qng.shape), _const_spec(kng.shape),
            tab, tab, tab,
        ],
        out_specs=[
            pl.BlockSpec((1, N_HEADS, HEAD_PAD, tm), lambda i, j: (i, 0, 0, j)),
            pl.BlockSpec((1, N_HEADS, tm, HEAD_PAD), lambda i, j: (i, 0, j, 0)),
            pl.BlockSpec((1, N_HEADS, V_DIM, tm), lambda i, j: (i, 0, 0, j)),
            tok(SSM_WIDTH), tok(2 * D_MODEL),
        ],
        out_shape=[
            jax.ShapeDtypeStruct((b, N_HEADS, HEAD_PAD, s), _BF16),
            jax.ShapeDtypeStruct((b, N_HEADS, s, HEAD_PAD), _BF16),
            jax.ShapeDtypeStruct((b, N_HEADS, V_DIM, s), _BF16),
            jax.ShapeDtypeStruct((b, s, SSM_WIDTH), _F32),
            jax.ShapeDtypeStruct((b, s, 2 * D_MODEL), _BF16),
        ],
        compiler_params=_params(("parallel", "parallel")),
        name="inproj",
    )(x, mod6, g1, win, qag, wuq, kvag, wukv, qng, kng, ctab, s1tab, s2tab)


def _attn_kernel(qt_ref, k_ref, vt_ref, o_ref, s_a, s_b, p_a, p_b, acc_ref, m_ref, al_ref, *, tk, n_kt):
    heads = range(2)
    ones = jnp.ones((ONES_ROWS, tk), _BF16)

    def scores(t, j, s_out):
        off = t * tk if isinstance(t, int) else pl.multiple_of(t * tk, tk)
        s_out[j] = jnp.dot(k_ref[0, j, pl.ds(off, tk), :], qt_ref[0, j], preferred_element_type=_F32)

    def weighted(t, j, p_in):
        off = t * tk if isinstance(t, int) else pl.multiple_of(t * tk, tk)
        lhs = jnp.concatenate([vt_ref[0, j, :, pl.ds(off, tk)], ones], axis=0)
        pv = jnp.dot(lhs, p_in[j], preferred_element_type=_F32)
        acc_ref[j] = acc_ref[j] * al_ref[j] + pv

    def softmax(j, s_in, p_out):
        s = s_in[j]
        m = m_ref[j]
        m_new = jnp.maximum(m, jnp.max(s, axis=0, keepdims=True))
        al_ref[j] = jnp.exp2(m - m_new)
        m_ref[j] = m_new
        p_out[j] = jnp.exp2((s - m_new).astype(_BF16))

    def step(t, s_cur, s_nxt, p_cur, p_prv):
        tp = max(t - 1, 0) if isinstance(t, int) else jnp.maximum(t - 1, 0)
        tn = min(t + 1, n_kt - 1) if isinstance(t, int) else jnp.minimum(t + 1, n_kt - 1)
        for j in heads:
            weighted(tp, j, p_prv)
        for j in heads:
            scores(tn, j, s_nxt)
        for j in heads:
            softmax(j, s_cur, p_cur)

    for j in heads:
        m_ref[j] = jnp.full(m_ref.shape[1:], -jnp.inf, _F32)
        al_ref[j] = jnp.ones(al_ref.shape[1:], _F32)
        acc_ref[j] = jnp.zeros(acc_ref.shape[1:], _F32)
        p_b[j] = jnp.zeros(p_b.shape[1:], _BF16)
        scores(0, j, s_a)

    def even_odd(t):
        step(t, s_a, s_b, p_a, p_b)
        step(t + 1, s_b, s_a, p_b, p_a)

    def unrolled(i, carry):
        for r in range(ATTN_UNROLL // 2):
            even_odd(ATTN_UNROLL * i + 2 * r)
        return carry

    n_loop = n_kt // ATTN_UNROLL
    lax.fori_loop(0, n_loop, unrolled, 0)
    for t in range(n_loop * ATTN_UNROLL, n_kt - 1, 2):
        even_odd(t)
    p_last = p_b
    if n_kt % 2:
        step(n_kt - 1, s_a, s_b, p_a, p_b)
        p_last = p_a
    outs = []
    for j in heads:
        weighted(n_kt - 1, j, p_last)
        acc = acc_ref[j]
        outs.append(acc[:V_DIM] / acc[V_DIM:V_DIM + 1])
    o_ref[0] = jnp.concatenate(outs, axis=0).T.astype(_BF16)


def _attn_call(qt, k, vt, tq, tk):
    b, h, _, s = qt.shape
    nk = k.shape[2]
    n_kt = nk // tk
    kern = functools.partial(_attn_kernel, tk=tk, n_kt=n_kt)
    return pl.pallas_call(
        kern,
        grid=(b, h // 2, s // tq),
        in_specs=[
            pl.BlockSpec((1, 2, HEAD_PAD, tq), lambda i, p, j: (i, p, 0, j)),
            pl.BlockSpec((1, 2, nk, HEAD_PAD), lambda i, p, j: (i, p, 0, 0)),
            pl.BlockSpec((1, 2, V_DIM, nk), lambda i, p, j: (i, p, 0, 0)),
        ],
        out_specs=pl.BlockSpec((1, tq, LANES), lambda i, p, j: (i, j, p)),
        out_shape=jax.ShapeDtypeStruct((b, s, h * V_DIM), _BF16),
        scratch_shapes=[
            pltpu.VMEM((2, tk, tq), _F32), pltpu.VMEM((2, tk, tq), _F32),
            pltpu.VMEM((2, tk, tq), _BF16), pltpu.VMEM((2, tk, tq), _BF16),
            pltpu.VMEM((2, V_DIM + ONES_ROWS, tq), _F32),
            pltpu.VMEM((2, 1, tq), _F32), pltpu.VMEM((2, 1, tq), _F32),
        ],
        compiler_params=_params(("parallel", "parallel", "arbitrary")),
        name="attn",
    )(qt, k, vt)


def _cmul(ar, ai, br, bi):
    return ar * br - ai * bi, ar * bi + ai * br


def _ssmprep_kernel(lre_ref, lim_ref, ldt_ref, btre_ref, btim_ref, cre_ref, cim_ref, dsk_ref,
                    wtre_ref, wtim_ref, cpre_ref, cpim_ref, k_ref, are_ref, aim_ref, *, reverse):
    lre = lre_ref[0]
    lim = lim_ref[0]
    dt = jnp.exp(ldt_ref[0])
    mag = jnp.exp(lre * dt)
    ar = mag * jnp.cos(lim * dt)
    ai = mag * jnp.sin(lim * dt)
    nr = ar - 1.0
    den = lre * lre + lim * lim
    cfr = (nr * lre + ai * lim) / den
    cfi = (ai * lre - nr * lim) / den
    bbr, bbi = _cmul(btre_ref[0], btim_ref[0], cfr, cfi)
    cre = cre_ref[0]
    cim = cim_ref[0]
    pr = [jnp.ones_like(ar)]
    pi = [jnp.zeros_like(ar)]
    for _ in range(SSM_CHUNK):
        r, i = _cmul(pr[-1], pi[-1], ar, ai)
        pr.append(r)
        pi.append(i)
    are_ref[0] = pr[SSM_CHUNK]
    aim_ref[0] = pi[SSM_CHUNK]
    wr, wi, cr, ci = [], [], [], []
    for k in range(SSM_CHUNK + 1):
        r, i = _cmul(cre, cim, pr[k], pi[k])
        cr.append(r)
        ci.append(i)
    for s in range(SSM_CHUNK):
        k = s if reverse else SSM_CHUNK - 1 - s
        r, i = _cmul(bbr, bbi, pr[k], pi[k])
        wr.append(r)
        wi.append(i)
    wtre_ref[0] = jnp.concatenate(wr, axis=0)
    wtim_ref[0] = jnp.concatenate(wi, axis=0)
    cpr = jnp.concatenate(cr, axis=0)
    cpi = jnp.concatenate(ci, axis=0)
    cpre_ref[0] = cpr
    cpim_ref[0] = cpi
    nk = SSM_CHUNK * SSM_GROUP
    dn = (((1,), (1,)), ((), ()))
    kk = (lax.dot_general(cpr[:nk], bbr, dn, precision=_HI, preferred_element_type=_F32)
          - lax.dot_general(cpi[:nk], bbi, dn, precision=_HI, preferred_element_type=_F32))
    if not reverse:
        row = lax.broadcasted_iota(jnp.int32, (nk, SSM_GROUP), 0)
        col = lax.broadcasted_iota(jnp.int32, (nk, SSM_GROUP), 1)
        kk = kk + jnp.where(row == col, dsk_ref[0], 0.0)
    k_ref[0] = kk


def _ssmprep_call(lre, lim, ldt, btre, btim, cre, cim, dsk, reverse):
    g, n = lre.shape
    c = SSM_GROUP
    l = SSM_CHUNK
    per_g = lambda *shape: pl.BlockSpec((1,) + shape, lambda i: (i,) + (0,) * len(shape))
    out = lambda *shape: jax.ShapeDtypeStruct((g,) + shape, _F32)
    return pl.pallas_call(
        functools.partial(_ssmprep_kernel, reverse=reverse),
        grid=(g,),
        in_specs=[per_g(1, n), per_g(1, n), per_g(1, n), per_g(c, n), per_g(c, n),
                  per_g(c, n), per_g(c, n), per_g(1, c)],
        out_specs=[per_g(l * c, n), per_g(l * c, n), per_g((l + 1) * c, n), per_g((l + 1) * c, n),
                   per_g(l * c, c), per_g(1, n), per_g(1, n)],
        out_shape=[out(l * c, n), out(l * c, n), out((l + 1) * c, n), out((l + 1) * c, n),
                   out(l * c, c), out(1, n), out(1, n)],
        compiler_params=_params(("arbitrary",)),
        name="ssmprep_b" if reverse else "ssmprep_f",
    )(lre.reshape(g, 1, n), lim.reshape(g, 1, n), jnp.broadcast_to(ldt[:, None, None], (g, 1, n)),
      btre, btim, cre, cim, dsk.reshape(g, 1, c))


def _ssm_tables(lam_re_f, lam_im_f, log_dt_f, c_re_f, c_im_f,
                lam_re_b, lam_im_b, log_dt_b, c_re_b, c_im_b, b_re, b_im, d_skip):
    g, n, c, l = SSM_GROUPS, SSM_STATE, SSM_GROUP, SSM_CHUNK
    btre = b_re.transpose(0, 2, 1)
    btim = b_im.transpose(0, 2, 1)
    dsk = d_skip.reshape(g, c)
    wfr, wfi, cfr, cfi, kf, afr, afi = _ssmprep_call(
        lam_re_f, lam_im_f, log_dt_f, btre, btim, c_re_f, c_im_f, dsk, False)
    wbr, wbi, cbr, cbi, kb, abr, abi = _ssmprep_call(
        lam_re_b, lam_im_b, log_dt_b, btre, btim, c_re_b, c_im_b, dsk, True)
    wt = jnp.concatenate([wfr, wbr, wfi, wbi], axis=-1).astype(_BF16)

    def readout(cp_f, cp_b):
        f = cp_f.reshape(g, l + 1, c, n)[:, 1:]
        b = cp_b.reshape(g, l + 1, c, n)[:, 1:][:, ::-1]
        f = f.reshape(g, l * c, n).transpose(0, 2, 1)
        b = b.reshape(g, l * c, n).transpose(0, 2, 1)
        return jnp.concatenate([f, b], axis=1)

    r = jnp.concatenate([readout(cfr, cbr), -readout(cfi, cbi)], axis=1).astype(_BF16)
    t_idx = jnp.arange(l)
    lag = t_idx[None, :] - t_idx[:, None]
    kf4 = kf.reshape(g, l, c, c)
    kb4 = kb.reshape(g, l, c, c)
    tf = jnp.where((lag >= 0)[None, :, :, None, None], kf4[:, jnp.clip(lag, 0, l - 1)], 0.0)
    tb = jnp.where((lag <= 0)[None, :, :, None, None], kb4[:, jnp.clip(-lag, 0, l - 1)], 0.0)
    tt = (tf + tb).transpose(0, 1, 4, 2, 3).reshape(g, l * c, l * c).astype(_BF16)
    a = jnp.stack([jnp.concatenate([afr, abr], axis=-1), jnp.concatenate([afi, abi], axis=-1)],
                  axis=1).reshape(g, 2, 2 * n)
    return wt, tt, r, a


def _ssm_kernel(u_ref, wt_ref, tt_ref, r_ref, a_ref, init_ref, *rest, nc, with_y):
    if with_y:
        y_ref, fin_ref, ere, eim, xfre, xfim, xbre, xbim = rest
    else:
        fin_ref, ere, eim, xfre, xfim, xbre, xbim = rest
    u = u_ref[0]
    e = jnp.dot(u, wt_ref[0], preferred_element_type=_F32)
    e = e.reshape(nc, SSM_ROWS, 2 * LANES)
    ere[...] = e[:, :, :LANES]
    eim[...] = e[:, :, LANES:]
    a = a_ref[0]
    a_re = jnp.broadcast_to(a[0:1, :], (SSM_ROWS, LANES))
    a_im = jnp.broadcast_to(a[1:2, :], (SSM_ROWS, LANES))
    fwd = lax.broadcasted_iota(jnp.int32, (SSM_ROWS, LANES), 1) < SSM_STATE

    def step(c, carry):
        x_re, x_im = carry
        cb = nc - 1 - c
        xfre[c] = x_re
        xfim[c] = x_im
        xbre[cb] = x_re
        xbim[cb] = x_im
        e_re = jnp.where(fwd, ere[c], ere[cb])
        e_im = jnp.where(fwd, eim[c], eim[cb])
        n_re = a_re * x_re - a_im * x_im + e_re
        n_im = a_re * x_im + a_im * x_re + e_im
        return n_re, n_im

    x_re, x_im = lax.fori_loop(0, nc, step, (init_ref[0, 0], init_ref[0, 1]))
    fin_ref[0, 0] = x_re
    fin_ref[0, 1] = x_im
    if with_y:
        fwd3 = lax.broadcasted_iota(jnp.int32, (nc, SSM_ROWS, LANES), 2) < SSM_STATE
        xin_re = jnp.where(fwd3, xfre[...], xbre[...]).reshape(nc * SSM_ROWS, LANES)
        xin_im = jnp.where(fwd3, xfim[...], xbim[...]).reshape(nc * SSM_ROWS, LANES)
        xin = jnp.concatenate([xin_re, xin_im], axis=-1).astype(_BF16)
        y = jnp.dot(u, tt_ref[0], preferred_element_type=_F32)
        y = y + jnp.dot(xin, r_ref[0], preferred_element_type=_F32)
        y_ref[0] = y


def _ssm_call(u_g, wt, tt, r, a, init, with_y):
    g, nr, w = u_g.shape
    nc = nr // SSM_ROWS
    per_g = lambda *shape: pl.BlockSpec((1,) + shape, lambda i: (i,) + (0,) * len(shape))
    fin_shape = jax.ShapeDtypeStruct((g, 2, SSM_ROWS, LANES), _F32)
    out_specs = [per_g(2, SSM_ROWS, LANES)]
    out_shape = [fin_shape]
    if with_y:
        out_specs = [per_g(nr, w)] + out_specs
        out_shape = [jax.ShapeDtypeStruct((g, nr, w), _F32)] + out_shape
    scratch = [pltpu.VMEM((nc, SSM_ROWS, LANES), _F32) for _ in range(6)]
    return pl.pallas_call(
        functools.partial(_ssm_kernel, nc=nc, with_y=with_y),
        grid=(g,),
        in_specs=[per_g(nr, w), per_g(w, w), per_g(w, w), per_g(w, w), per_g(2, LANES),
                  per_g(2, SSM_ROWS, LANES)],
        out_specs=out_specs,
        out_shape=out_shape,
        scratch_shapes=scratch,
        compiler_params=_params(("arbitrary",)),
        name="ssm_lat" if with_y else "ssm_ctx",
    )(u_g, wt, tt, r, a, init)


def _to_groups(u):
    b, s, _ = u.shape
    nc = s // SSM_CHUNK
    ug = u.astype(_BF16).reshape(b, nc, SSM_CHUNK, SSM_GROUPS, SSM_GROUP).transpose(3, 1, 0, 2, 4)
    ug = ug.reshape(SSM_GROUPS, nc, b, SSM_CHUNK * SSM_GROUP)
    ug = jnp.pad(ug, ((0, 0), (0, 0), (0, SSM_ROWS - b), (0, 0)))
    return ug.reshape(SSM_GROUPS, nc * SSM_ROWS, SSM_CHUNK * SSM_GROUP)


def _from_groups(y, b):
    g, nr, _ = y.shape
    nc = nr // SSM_ROWS
    y = y.reshape(g, nc, SSM_ROWS, SSM_CHUNK, SSM_GROUP)[:, :, :b]
    return y.transpose(2, 1, 3, 0, 4).reshape(b, nc * SSM_CHUNK, SSM_WIDTH)


def _mix_kernel(x_ref, mod_ref, o_ref, y_ref, sg_ref, wo_ref, wglu_ref, wout_ref, x1_ref):
    a = jnp.dot(o_ref[0], wo_ref[...], preferred_element_type=_F32)
    yg = jax.nn.gelu(y_ref[0])
    glu = jnp.dot(yg.astype(_BF16), wglu_ref[...], preferred_element_type=_F32)
    s = glu[:, :D_MODEL] * jax.nn.sigmoid(glu[:, D_MODEL:])
    sg = sg_ref[0].astype(_F32)
    merged = sg[:, :D_MODEL] * a + sg[:, D_MODEL:] * s
    out = jnp.dot(merged.astype(_BF16), wout_ref[...], preferred_element_type=_F32)
    g1 = mod_ref[0][2:3, :]
    x1_ref[0] = x_ref[0] + g1 * out


def _mix_call(x, mod6, o, y, sg, wo, wglu, wout, tm):
    b, s, d = x.shape
    tok = lambda w: pl.BlockSpec((1, tm, w), lambda i, j: (i, j, 0))
    return pl.pallas_call(
        _mix_kernel,
        grid=(b, s // tm),
        in_specs=[tok(d), pl.BlockSpec((1, 6, d), lambda i, j: (i, 0, 0)),
                  tok(o.shape[2]), tok(y.shape[2]), tok(sg.shape[2]),
                  _const_spec(wo.shape), _const_spec(wglu.shape), _const_spec(wout.shape)],
        out_specs=tok(d),
        out_shape=jax.ShapeDtypeStruct((b, s, d), _F32),
        compiler_params=_params(("parallel", "parallel")),
        name="mix",
    )(x, mod6, o, y, sg, wo, wglu, wout)


def _ffn_kernel(x_ref, prev_ref, next_ref, mod_ref, g2_ref, wup_ref, cw_ref, cb_ref, wdn_ref,
                out_ref, acc_ref, *, n_chunks):
    j = pl.program_id(1)
    nj = pl.num_programs(1)
    tm = x_ref.shape[1]
    mod = mod_ref[0]
    sh2 = mod[3:4, :]
    sc2 = mod[4:5, :]
    g2 = mod[5:6, :]
    gain = g2_ref[...]

    def prenorm(v):
        return ((_rms(v, D_MODEL) * gain) * (1.0 + sc2) + sh2).astype(_BF16)

    x = x_ref[0]
    h = prenorm(x)
    hh = prenorm(jnp.concatenate([prev_ref[0], next_ref[0]], axis=0))
    has_prev = (j > 0).astype(_F32)
    has_next = (j < nj - 1).astype(_F32)
    row = lax.broadcasted_iota(jnp.int32, (tm, 2 * FFN_CHUNK), 0)
    acc_ref[...] = jnp.zeros_like(acc_ref)

    def body(ci, carry):
        co = pl.multiple_of(ci * 2 * FFN_CHUNK, 2 * FFN_CHUNK)
        ro = pl.multiple_of(ci * FFN_CHUNK, FFN_CHUNK)
        w = wup_ref[:, pl.ds(co, 2 * FFN_CHUNK)]
        p = jnp.dot(h, w, preferred_element_type=_F32)
        ph = jnp.dot(hh, w, preferred_element_type=_F32)
        before = jnp.where(row == 0, ph[7:8, :] * has_prev, pltpu.roll(p, 1, 0))
        after = jnp.where(row == tm - 1, ph[8:9, :] * has_next, pltpu.roll(p, tm - 1, 0))
        cw = cw_ref[:, pl.ds(co, 2 * FFN_CHUNK)]
        uc = before * cw[0:1, :] + p * cw[1:2, :] + after * cw[2:3, :] + cb_ref[:, pl.ds(co, 2 * FFN_CHUNK)]
        val = uc[:, :FFN_CHUNK]
        gate = uc[:, FFN_CHUNK:]
        act = (gate * jax.nn.sigmoid(gate) * val).astype(_BF16)
        acc_ref[...] += jnp.dot(act, wdn_ref[pl.ds(ro, FFN_CHUNK), :], preferred_element_type=_F32)
        return carry

    lax.fori_loop(0, n_chunks, body, 0)
    out_ref[0] = x + g2 * acc_ref[...]


def _ffn_call(x1, mod6, g2, wup, cw, cb, wdn, tm):
    b, s, d = x1.shape
    n_chunks = wdn.shape[0] // FFN_CHUNK
    hb = tm // 8
    last = s // 8 - 1
    return pl.pallas_call(
        functools.partial(_ffn_kernel, n_chunks=n_chunks),
        grid=(b, s // tm),
        in_specs=[
            pl.BlockSpec((1, tm, d), lambda i, j: (i, j, 0)),
            pl.BlockSpec((1, 8, d), lambda i, j: (i, jnp.maximum(j * hb - 1, 0), 0)),
            pl.BlockSpec((1, 8, d), lambda i, j: (i, jnp.minimum((j + 1) * hb, last), 0)),
            pl.BlockSpec((1, 6, d), lambda i, j: (i, 0, 0)),
            _const_spec(g2.shape), _const_spec(wup.shape), _const_spec(cw.shape),
            _const_spec(cb.shape), _const_spec(wdn.shape),
        ],
        out_specs=pl.BlockSpec((1, tm, d), lambda i, j: (i, j, 0)),
        out_shape=jax.ShapeDtypeStruct((b, s, d), _F32),
        scratch_shapes=[pltpu.VMEM((tm, d), _F32)],
        compiler_params=_params(("parallel", "arbitrary")),
        name="ffn",
    )(x1, x1, x1, mod6, g2, wup, cw, cb, wdn)


def _rope_tables(s):
    rows = s // GRID_W
    row = jnp.repeat(jnp.arange(rows), GRID_W)
    col = jnp.tile(jnp.arange(GRID_W), rows)
    pairs = QK_ROPE // 4
    freqs = ROPE_THETA ** (-jnp.arange(pairs, dtype=_F32) / pairs)
    ang = jnp.concatenate([row[:, None] * freqs, col[:, None] * freqs], axis=-1)
    cos, sin = jnp.cos(ang), jnp.sin(ang)
    half = QK_ROPE // 2
    z = lambda w: jnp.zeros((s, w), _F32)
    ctab = jnp.concatenate([jnp.ones((s, QK_NOPE), _F32), cos, cos, z(HEAD_PAD - QK_DIM)], axis=-1)
    s1 = jnp.concatenate([z(QK_NOPE), -sin, z(HEAD_PAD - QK_NOPE - half)], axis=-1)
    s2 = jnp.concatenate([z(QK_NOPE + half), sin, z(HEAD_PAD - QK_DIM)], axis=-1)
    return ctab, s1, s2


def _identity_tables(s):
    ctab = jnp.concatenate([jnp.ones((s, QK_DIM), _F32), jnp.zeros((s, HEAD_PAD - QK_DIM), _F32)], axis=-1)
    z = jnp.zeros((s, HEAD_PAD), _F32)
    return ctab, z, z


def _pick_tile(n, pref):
    t = min(n, pref)
    while n % t:
        t //= 2
    return t


def kernel(x, c, ctx, c_ctx, w_mod, b_mod, norm1_g, norm2_g, w_in, q_a_g, w_uq, kv_a_g, w_ukv, q_norm_g, k_norm_g, w_o_attn, lam_re_f, lam_im_f, log_dt_f, c_re_f, c_im_f, lam_re_b, lam_im_b, log_dt_b, c_re_b, c_im_b, b_re, b_im, d_skip, w_glu, w_out, w_up, conv_w, conv_b, w_down):
    b, s, d = x.shape
    n_ctx = ctx.shape[1]
    depth = w_mod.shape[0]
    assert depth == 1, "context update between layers is not implemented"
    l = 0

    wi = w_in[l]
    o1, o2, o3, o4 = Q_LORA, Q_LORA + KV_LORA, Q_LORA + KV_LORA + QK_ROPE, Q_LORA + KV_LORA + QK_ROPE + SSM_WIDTH
    zc = lambda w: jnp.zeros((d, w), wi.dtype)
    win = jnp.concatenate([wi[:, :o1], wi[:, o1:o2], wi[:, o3:o4], wi[:, o4:],
                           zc(QK_NOPE), wi[:, o2:o3], zc(HEAD_PAD - QK_DIM)], axis=-1).astype(_BF16)
    wuq = jnp.pad(w_uq[l].reshape(Q_LORA, N_HEADS, QK_DIM), ((0, 0), (0, 0), (0, HEAD_PAD - QK_DIM)))
    wuq = wuq.reshape(Q_LORA, N_HEADS * HEAD_PAD).astype(_BF16)
    wkv = w_ukv[l].reshape(KV_LORA, N_HEADS, QK_NOPE + V_DIM)
    wk = jnp.pad(wkv[:, :, :QK_NOPE], ((0, 0), (0, 0), (0, HEAD_PAD - QK_NOPE))).reshape(KV_LORA, N_HEADS * HEAD_PAD)
    wv = wkv[:, :, QK_NOPE:].reshape(KV_LORA, N_HEADS * V_DIM)
    wukv = jnp.concatenate([wk, wv], axis=-1).astype(_BF16)
    padg = lambda g: jnp.pad(g, (0, HEAD_PAD - QK_DIM)).reshape(1, HEAD_PAD)
    qng, kng = padg(q_norm_g[l]), padg(k_norm_g[l])
    g1 = norm1_g[l].reshape(1, d)
    g2 = norm2_g[l].reshape(1, d)
    qag = q_a_g[l].reshape(1, Q_LORA)
    kvag = kv_a_g[l].reshape(1, KV_LORA)
    nch = FFN_HIDDEN // FFN_CHUNK
    pair = lambda w: jnp.concatenate(
        [w[..., :FFN_HIDDEN].reshape(w.shape[:-1] + (nch, 1, FFN_CHUNK)),
         w[..., FFN_HIDDEN:].reshape(w.shape[:-1] + (nch, 1, FFN_CHUNK))], axis=-2
    ).reshape(w.shape[:-1] + (2 * FFN_HIDDEN,))
    wup = pair(w_up[l]).astype(_BF16)
    cw = pair(conv_w[l])
    cb = pair(conv_b[l].reshape(1, 2 * FFN_HIDDEN))
    wdn = w_down[l].astype(_BF16)
    wo = w_o_attn[l].astype(_BF16)
    wglu = w_glu[l].astype(_BF16)
    wout = w_out[l].astype(_BF16)

    cc = jnp.concatenate([c, c_ctx[None, :], jnp.zeros((8 - b - 1, d), c.dtype)], axis=0)
    mod = _mod_call(cc, w_mod[l], b_mod[l])
    mod_lat = mod[:b].reshape(b, 6, d)
    mod_ctx = jnp.broadcast_to(mod[b].reshape(1, 6, d), (b, 6, d))

    tm = _pick_tile(s, 512)
    tmc = _pick_tile(n_ctx, 512)
    shared = (g1, win, qag, wuq, kvag, wukv, qng, kng)
    qt, k, vt, u, sg = _inproj_call(x, mod_lat, *shared, *_rope_tables(s), tm)
    _, k_c, vt_c, u_c, _ = _inproj_call(ctx, mod_ctx, *shared, *_identity_tables(n_ctx), tmc)

    wt, tt, r, a = _ssm_tables(lam_re_f[l], lam_im_f[l], log_dt_f[l], c_re_f[l], c_im_f[l],
                               lam_re_b[l], lam_im_b[l], log_dt_b[l], c_re_b[l], c_im_b[l],
                               b_re[l], b_im[l], d_skip[l])
    zero_init = jnp.zeros((SSM_GROUPS, 2, SSM_ROWS, LANES), _F32)
    (fin_c,) = _ssm_call(_to_groups(u_c), wt, tt, r, a, zero_init, False)
    y_g, _ = _ssm_call(_to_groups(u), wt, tt, r, a, fin_c, True)
    y = _from_groups(y_g, b)

    k_all = jnp.concatenate([k, k_c], axis=2)
    vt_all = jnp.concatenate([vt, vt_c], axis=-1)
    tq = _pick_tile(s, 512)
    tk = _pick_tile(s + n_ctx, 256)
    o = _attn_call(qt, k_all, vt_all, tq, tk)

    x1 = _mix_call(x, mod_lat, o, y, sg, wo, wglu, wout, tm)
    return _ffn_call(x1, mod_lat, g2, wup, cw, cb, wdn, tm)
```

```python
import functools
import math

import jax
import jax.numpy as jnp
from jax import lax
from jax.experimental import pallas as pl
from jax.experimental.pallas import tpu as pltpu

D_MODEL = 1024
GRID_W = 64
N_HEADS = 8
QK_NOPE = 64
QK_ROPE = 32
QK_DIM = QK_NOPE + QK_ROPE
V_DIM = 64
Q_LORA = 384
KV_LORA = 256
ROPE_THETA = 10000.0
SSM_WIDTH = 512
SSM_GROUP = 16
SSM_GROUPS = SSM_WIDTH // SSM_GROUP
SSM_STATE = 64
FFN_HIDDEN = 2816
EPS = 1e-6

LANES = 128
HEAD_PAD = LANES
ONES_ROWS = 16
INPROJ_SUB = 128
SSM_CHUNK = 16
SSM_ROWS = 8
SSM_BLOCK = 8
SSM_SEG_CHUNKS = 64
FFN_CHUNK = 256
VMEM_LIMIT = 56 * 1024 * 1024

_HI = lax.Precision.HIGHEST
_F32 = jnp.float32
_BF16 = jnp.bfloat16


def _params(sem):
    return pltpu.CompilerParams(dimension_semantics=sem, vmem_limit_bytes=VMEM_LIMIT)


def _const_spec(shape):
    nd = len(shape)
    return pl.BlockSpec(shape, lambda *_: (0,) * nd)


def _rms(v, width):
    return v * lax.rsqrt(jnp.sum(v * v, axis=-1, keepdims=True) * (1.0 / width) + EPS)


def _mod_kernel(c_ref, w_ref, b_ref, o_ref):
    c = c_ref[...]
    s = c * jax.nn.sigmoid(c)
    o_ref[...] = jnp.dot(s, w_ref[...], precision=_HI, preferred_element_type=_F32) + b_ref[...]


def _mod_call(cc, w_mod, b_mod):
    rows, d = cc.shape
    n = w_mod.shape[1]
    tn = 1024
    return pl.pallas_call(
        _mod_kernel,
        grid=(n // tn,),
        in_specs=[
            pl.BlockSpec((rows, d), lambda j: (0, 0)),
            pl.BlockSpec((d, tn), lambda j: (0, j)),
            pl.BlockSpec((1, tn), lambda j: (0, j)),
        ],
        out_specs=pl.BlockSpec((rows, tn), lambda j: (0, j)),
        out_shape=jax.ShapeDtypeStruct((rows, n), _F32),
        compiler_params=_params(("arbitrary",)),
        name="mod",
    )(cc, w_mod, b_mod.reshape(1, n))


_C_CQ = 0
_C_CKV = _C_CQ + Q_LORA
_C_U = _C_CKV + KV_LORA
_C_GL = _C_U + SSM_WIDTH
_C_KR = _C_GL + 2 * D_MODEL
_C_KRP = _C_KR + LANES
_IN_W = _C_KRP + LANES
_QW = N_HEADS * HEAD_PAD


def _inproj_kernel(x_ref, mod_ref, g1_ref, win_ref, qag_ref, wuq_ref, kvag_ref, wukv_ref,
                   qng_ref, kng_ref, ctab_ref, stab_ref,
                   qt_ref, k_ref, vt_ref, u_ref, sg_ref, *, sub):
    tm = x_ref.shape[1]
    mod = mod_ref[0]
    sh1 = mod[0:1, :]
    sc1 = 1.0 + mod[1:2, :]
    g1 = g1_ref[...]
    qg = qng_ref[...] * (QK_DIM ** -0.5 * math.log2(math.e))
    kg = kng_ref[...]

    def project(i):
        x = x_ref[0, i * sub:(i + 1) * sub, :]
        h = (_rms(x, D_MODEL) * g1) * sc1 + sh1
        return jnp.dot(h.astype(_BF16), win_ref[...], preferred_element_type=_F32)

    def expand(i, proj):
        rows = slice(i * sub, (i + 1) * sub)
        u_ref[0, rows, :] = proj[:, _C_U:_C_GL].astype(_BF16)
        sg_ref[0, rows, :] = jax.nn.sigmoid(proj[:, _C_GL:_C_KR]).astype(_BF16)
        cq = _rms(proj[:, _C_CQ:_C_CKV], Q_LORA) * qag_ref[...]
        qall = jnp.dot(cq.astype(_BF16), wuq_ref[...], preferred_element_type=_F32)
        ckv = _rms(proj[:, _C_CKV:_C_U], KV_LORA) * kvag_ref[...]
        kvall = jnp.dot(ckv.astype(_BF16), wukv_ref[...], preferred_element_type=_F32)
        return qall, kvall, proj[:, _C_KR:_C_KRP], proj[:, _C_KRP:_IN_W]

    def heads(i, qall, kvall, kr, krp):
        rows = slice(i * sub, (i + 1) * sub)
        ctab = ctab_ref[rows, :]
        stab = stab_ref[rows, :]
        qc = qg[0:1] * ctab
        qs = qg[1:2] * stab
        kc = kg[0:1] * ctab
        k_rot = krp * (kg[1:2] * stab)
        for hd in range(N_HEADS):
            lo = hd * HEAD_PAD
            qh = qall[:, lo:lo + HEAD_PAD]
            nq = lax.rsqrt(jnp.sum(qh * qh, axis=-1, keepdims=True) * (1.0 / QK_DIM) + EPS)
            qr = (qh * qc + qall[:, _QW + lo:_QW + lo + HEAD_PAD] * qs) * nq
            qt_ref[0, hd, :, rows] = qr.T.astype(_BF16)
            kh = kvall[:, lo:lo + HEAD_PAD] + kr
            nk = lax.rsqrt(jnp.sum(kh * kh, axis=-1, keepdims=True) * (1.0 / QK_DIM) + EPS)
            k_ref[0, hd, rows, :] = ((kh * kc + k_rot) * nk).astype(_BF16)
        for hp in range(N_HEADS // 2):
            lo = _QW + hp * LANES
            vt = kvall[:, lo:lo + LANES].T.astype(_BF16)
            vt_ref[0, 2 * hp, :, rows] = vt[:V_DIM]
            vt_ref[0, 2 * hp + 1, :, rows] = vt[V_DIM:]

    n_sub = tm // sub
    proj = project(0)
    for i in range(n_sub):
        nxt = project(i + 1) if i + 1 < n_sub else None
        heads(i, *expand(i, proj))
        proj = nxt


def _inproj_call(x, mod6, g1, win, qag, wuq, kvag, wukv, qng, kng, ctab, stab, tm):
    b, s, d = x.shape
    grid = (b, s // tm)
    sub = min(tm, INPROJ_SUB)
    tok = lambda w: pl.BlockSpec((1, tm, w), lambda i, j: (i, j, 0))
    tab = pl.BlockSpec((tm, LANES), lambda i, j: (j, 0))
    return pl.pallas_call(
        functools.partial(_inproj_kernel, sub=sub),
        grid=grid,
        in_specs=[
            tok(d),
            pl.BlockSpec((1, 6, d), lambda i, j: (i, 0, 0)),
            _const_spec(g1.shape), _const_spec(win.shape), _const_spec(qag.shape),
            _const_spec(wuq.shape), _const_spec(kvag.shape), _const_spec(wukv.shape),
            _const_spec(qng.shape), _const_spec(kng.shape),
            tab, tab,
        ],
        out_specs=[
            pl.BlockSpec((1, N_HEADS, HEAD_PAD, tm), lambda i, j: (i, 0, 0, j)),
            pl.BlockSpec((1, N_HEADS, tm, HEAD_PAD), lambda i, j: (i, 0, j, 0)),
            pl.BlockSpec((1, N_HEADS, V_DIM, tm), lambda i, j: (i, 0, 0, j)),
            tok(SSM_WIDTH), tok(2 * D_MODEL),
        ],
        out_shape=[
            jax.ShapeDtypeStruct((b, N_HEADS, HEAD_PAD, s), _BF16),
            jax.ShapeDtypeStruct((b, N_HEADS, s, HEAD_PAD), _BF16),
            jax.ShapeDtypeStruct((b, N_HEADS, V_DIM, s), _BF16),
            jax.ShapeDtypeStruct((b, s, SSM_WIDTH), _BF16),
            jax.ShapeDtypeStruct((b, s, 2 * D_MODEL), _BF16),
        ],
        compiler_params=_params(("parallel", "parallel")),
        name="inproj",
    )(x, mod6, g1, win, qag, wuq, kvag, wukv, qng, kng, ctab, stab)


def _attn_kernel(qt_ref, k_ref, vt_ref, o_ref, s_a, s_b, c_a, c_b, p_a, p_b, acc_ref, m_ref, al_ref,
                 *, tk, n_kt):
    heads = range(2)
    ones = jnp.ones((ONES_ROWS, tk), _BF16)
    even = (s_a, c_a, p_a)
    odd = (s_b, c_b, p_b)

    def scores(t, j, dst):
        s = jnp.dot(k_ref[0, j, t * tk:(t + 1) * tk, :], qt_ref[0, j], preferred_element_type=_F32)
        dst[0][j] = s
        dst[1][j] = jnp.max(s, axis=0, keepdims=True)

    def weighted(t, j, p_in):
        lhs = jnp.concatenate([vt_ref[0, j, :, t * tk:(t + 1) * tk], ones], axis=0)
        pv = jnp.dot(lhs, p_in[j], preferred_element_type=_F32)
        acc_ref[j] = acc_ref[j] * al_ref[j] + pv

    def softmax(j, cur):
        m = m_ref[j]
        m_new = jnp.maximum(m, cur[1][j])
        al_ref[j] = jnp.exp2(m - m_new)
        m_ref[j] = m_new
        cur[2][j] = jnp.exp2((cur[0][j] - m_new).astype(_BF16))

    def step(t, cur, nxt):
        for j in heads:
            weighted(max(t - 1, 0), j, nxt[2])
        for j in heads:
            scores(min(t + 1, n_kt - 1), j, nxt)
        for j in heads:
            softmax(j, cur)

    for j in heads:
        m_ref[j] = jnp.full(m_ref.shape[1:], -jnp.inf, _F32)
        al_ref[j] = jnp.ones(al_ref.shape[1:], _F32)
        acc_ref[j] = jnp.zeros(acc_ref.shape[1:], _F32)
        p_b[j] = jnp.zeros(p_b.shape[1:], _BF16)
        scores(0, j, even)

    for t in range(n_kt):
        if t % 2 == 0:
            step(t, even, odd)
        else:
            step(t, odd, even)
    p_last = p_a if n_kt % 2 else p_b
    outs = []
    for j in heads:
        weighted(n_kt - 1, j, p_last)
        acc = acc_ref[j]
        outs.append(acc[:V_DIM] / acc[V_DIM:V_DIM + 1])
    o_ref[0] = jnp.concatenate(outs, axis=0).T.astype(_BF16)


def _attn_call(qt, k, vt, tq, tk):
    b, h, _, s = qt.shape
    nk = k.shape[2]
    n_kt = nk // tk
    kern = functools.partial(_attn_kernel, tk=tk, n_kt=n_kt)
    return pl.pallas_call(
        kern,
        grid=(b, h // 2, s // tq),
        in_specs=[
            pl.BlockSpec((1, 2, HEAD_PAD, tq), lambda i, p, j: (i, p, 0, j)),
            pl.BlockSpec((1, 2, nk, HEAD_PAD), lambda i, p, j: (i, p, 0, 0)),
            pl.BlockSpec((1, 2, V_DIM, nk), lambda i, p, j: (i, p, 0, 0)),
        ],
        out_specs=pl.BlockSpec((1, tq, LANES), lambda i, p, j: (i, j, p)),
        out_shape=jax.ShapeDtypeStruct((b, s, h * V_DIM), _BF16),
        scratch_shapes=[
            pltpu.VMEM((2, tk, tq), _F32), pltpu.VMEM((2, tk, tq), _F32),
            pltpu.VMEM((2, 1, tq), _F32), pltpu.VMEM((2, 1, tq), _F32),
            pltpu.VMEM((2, tk, tq), _BF16), pltpu.VMEM((2, tk, tq), _BF16),
            pltpu.VMEM((2, V_DIM + ONES_ROWS, tq), _F32),
            pltpu.VMEM((2, 1, tq), _F32), pltpu.VMEM((2, 1, tq), _F32),
        ],
        compiler_params=_params(("parallel", "parallel", "arbitrary")),
        name="attn",
    )(qt, k, vt)


def _cmul(ar, ai, br, bi):
    return ar * br - ai * bi, ar * bi + ai * br


def _ssmprep_kernel(lre_ref, lim_ref, ldt_ref, btre_ref, btim_ref, cre_ref, cim_ref, dsk_ref,
                    wtre_ref, wtim_ref, cpre_ref, cpim_ref, k_ref, are_ref, aim_ref, *, reverse):
    lre = lre_ref[0]
    lim = lim_ref[0]
    dt = jnp.exp(ldt_ref[0])
    mag = jnp.exp(lre * dt)
    ar = mag * jnp.cos(lim * dt)
    ai = mag * jnp.sin(lim * dt)
    nr = ar - 1.0
    den = lre * lre + lim * lim
    cfr = (nr * lre + ai * lim) / den
    cfi = (ai * lre - nr * lim) / den
    bbr, bbi = _cmul(btre_ref[0], btim_ref[0], cfr, cfi)
    cre = cre_ref[0]
    cim = cim_ref[0]
    pr = [jnp.ones_like(ar)]
    pi = [jnp.zeros_like(ar)]
    for _ in range(SSM_CHUNK):
        r, i = _cmul(pr[-1], pi[-1], ar, ai)
        pr.append(r)
        pi.append(i)
    are_ref[0] = pr[SSM_CHUNK]
    aim_ref[0] = pi[SSM_CHUNK]
    wr, wi, cr, ci = [], [], [], []
    for k in range(SSM_CHUNK + 1):
        r, i = _cmul(cre, cim, pr[k], pi[k])
        cr.append(r)
        ci.append(i)
    for s in range(SSM_CHUNK):
        k = s if reverse else SSM_CHUNK - 1 - s
        r, i = _cmul(bbr, bbi, pr[k], pi[k])
        wr.append(r)
        wi.append(i)
    wtre_ref[0] = jnp.concatenate(wr, axis=0)
    wtim_ref[0] = jnp.concatenate(wi, axis=0)
    cpr = jnp.concatenate(cr, axis=0)
    cpi = jnp.concatenate(ci, axis=0)
    cpre_ref[0] = cpr
    cpim_ref[0] = cpi
    nk = SSM_CHUNK * SSM_GROUP
    dn = (((1,), (1,)), ((), ()))
    kk = (lax.dot_general(cpr[:nk], bbr, dn, precision=_HI, preferred_element_type=_F32)
          - lax.dot_general(cpi[:nk], bbi, dn, precision=_HI, preferred_element_type=_F32))
    if not reverse:
        row = lax.broadcasted_iota(jnp.int32, (nk, SSM_GROUP), 0)
        col = lax.broadcasted_iota(jnp.int32, (nk, SSM_GROUP), 1)
        kk = kk + jnp.where(row == col, dsk_ref[0], 0.0)
    k_ref[0] = kk


def _ssmprep_call(lre, lim, ldt, btre, btim, cre, cim, dsk, reverse):
    g, n = lre.shape
    c = SSM_GROUP
    l = SSM_CHUNK
    per_g = lambda *shape: pl.BlockSpec((1,) + shape, lambda i: (i,) + (0,) * len(shape))
    out = lambda *shape: jax.ShapeDtypeStruct((g,) + shape, _F32)
    return pl.pallas_call(
        functools.partial(_ssmprep_kernel, reverse=reverse),
        grid=(g,),
        in_specs=[per_g(1, n), per_g(1, n), per_g(1, n), per_g(c, n), per_g(c, n),
                  per_g(c, n), per_g(c, n), per_g(1, c)],
        out_specs=[per_g(l * c, n), per_g(l * c, n), per_g((l + 1) * c, n), per_g((l + 1) * c, n),
                   per_g(l * c, c), per_g(1, n), per_g(1, n)],
        out_shape=[out(l * c, n), out(l * c, n), out((l + 1) * c, n), out((l + 1) * c, n),
                   out(l * c, c), out(1, n), out(1, n)],
        compiler_params=_params(("arbitrary",)),
        name="ssmprep_b" if reverse else "ssmprep_f",
    )(lre.reshape(g, 1, n), lim.reshape(g, 1, n), jnp.broadcast_to(ldt[:, None, None], (g, 1, n)),
      btre, btim, cre, cim, dsk.reshape(g, 1, c))


def _ssm_tables(lam_re_f, lam_im_f, log_dt_f, c_re_f, c_im_f,
                lam_re_b, lam_im_b, log_dt_b, c_re_b, c_im_b, b_re, b_im, d_skip):
    g, n, c, l, gb = SSM_GROUPS, SSM_STATE, SSM_GROUP, SSM_CHUNK, SSM_BLOCK
    nb = g // gb
    btre = b_re.transpose(0, 2, 1)
    btim = b_im.transpose(0, 2, 1)
    dsk = d_skip.reshape(g, c)
    wfr, wfi, cfr, cfi, kf, afr, afi = _ssmprep_call(
        lam_re_f, lam_im_f, log_dt_f, btre, btim, c_re_f, c_im_f, dsk, False)
    wbr, wbi, cbr, cbi, kb, abr, abi = _ssmprep_call(
        lam_re_b, lam_im_b, log_dt_b, btre, btim, c_re_b, c_im_b, dsk, True)
    eye = jnp.eye(gb, dtype=_F32)

    def local(w):
        w = w.reshape(nb, gb, l, c, n)
        return jnp.einsum("jgscn,gh->jsgchn", w, eye).reshape(nb, l * gb * c, gb * n)

    wtf = jnp.concatenate([local(wfr), local(wfi)], axis=-1).astype(_BF16)
    wtb = jnp.concatenate([local(wbr), local(wbi)], axis=-1).astype(_BF16)

    def readout(cp, reverse):
        cp = cp.reshape(g, l + 1, c, n)[:, 1:]
        if reverse:
            cp = cp[:, ::-1]
        cp = cp.reshape(nb, gb, l, c, n)
        return jnp.einsum("jgtdn,gh->jgnthd", cp, eye).reshape(nb, gb * n, l * gb * c)

    rf = jnp.concatenate([readout(cfr, False), -readout(cfi, False)], axis=1).astype(_BF16)
    rb = jnp.concatenate([readout(cbr, True), -readout(cbi, True)], axis=1).astype(_BF16)
    t_idx = jnp.arange(l)
    lag = t_idx[None, :] - t_idx[:, None]
    kf4 = kf.reshape(g, l, c, c)
    kb4 = kb.reshape(g, l, c, c)
    tf = jnp.where((lag >= 0)[None, :, :, None, None], kf4[:, jnp.clip(lag, 0, l - 1)], 0.0)
    tb = jnp.where((lag <= 0)[None, :, :, None, None], kb4[:, jnp.clip(-lag, 0, l - 1)], 0.0)
    tt = (tf + tb).reshape(nb, gb, l, l, c, c)
    tt = jnp.einsum("jgstdc,gh->jsgcthd", tt, eye).reshape(nb, l * gb * c, l * gb * c).astype(_BF16)
    a = jnp.stack([v.reshape(nb, gb * n) for v in (afr, afi, abr, abi)], axis=1)
    return wtf, wtb, tt, rf, rb, a


def _ssm_scan_kernel(uf_ref, ub_ref, wtf_ref, wtb_ref, a_ref, init_ref, xf_ref, xb_ref, fin_ref,
                     ef, eb, st, *, segc):
    i = pl.program_id(1)
    half = SSM_BLOCK * SSM_STATE

    @pl.when(i == 0)
    def _():
        st[...] = init_ref[0]

    ef[...] = jnp.dot(uf_ref[...], wtf_ref[0], preferred_element_type=_F32)
    eb[...] = jnp.dot(ub_ref[...], wtb_ref[0], preferred_element_type=_F32)
    a = a_ref[0]
    afr, afi, abr, abi = (jnp.broadcast_to(a[k:k + 1, :], (SSM_ROWS, half)) for k in range(4))

    def step(c, carry):
        fr, fi, br, bi = carry
        rf = pl.ds(pl.multiple_of(c * SSM_ROWS, SSM_ROWS), SSM_ROWS)
        rb = pl.ds(pl.multiple_of((segc - 1 - c) * SSM_ROWS, SSM_ROWS), SSM_ROWS)
        xf_ref[0, rf, :] = jnp.concatenate([fr, fi], axis=-1)
        xb_ref[0, rb, :] = jnp.concatenate([br, bi], axis=-1)
        e_f = ef[rf, :]
        e_b = eb[rb, :]
        nfr = afr * fr - afi * fi + e_f[:, :half]
        nfi = afr * fi + afi * fr + e_f[:, half:]
        nbr = abr * br - abi * bi + e_b[:, :half]
        nbi = abr * bi + abi * br + e_b[:, half:]
        return nfr, nfi, nbr, nbi

    fin = lax.fori_loop(0, segc, step, tuple(st[k] for k in range(4)))
    for k in range(4):
        st[k] = fin[k]
        fin_ref[0, k] = fin[k]


def _ssm_scan_call(ucat, wtf, wtb, a, init):
    rows, width = ucat.shape
    nb = wtf.shape[0]
    kw = width // nb
    half = SSM_BLOCK * SSM_STATE
    nc = rows // SSM_ROWS
    segc = min(nc, SSM_SEG_CHUNKS)
    nseg = nc // segc
    seg_rows = segc * SSM_ROWS
    f32 = lambda *shape: jax.ShapeDtypeStruct(shape, _F32)
    return pl.pallas_call(
        functools.partial(_ssm_scan_kernel, segc=segc),
        grid=(nb, nseg),
        in_specs=[
            pl.BlockSpec((seg_rows, kw), lambda j, i: (i, j)),
            pl.BlockSpec((seg_rows, kw), lambda j, i: (nseg - 1 - i, j)),
            pl.BlockSpec((1, kw, 2 * half), lambda j, i: (j, 0, 0)),
            pl.BlockSpec((1, kw, 2 * half), lambda j, i: (j, 0, 0)),
            pl.BlockSpec((1, 4, half), lambda j, i: (j, 0, 0)),
            pl.BlockSpec((1, 4, SSM_ROWS, half), lambda j, i: (j, 0, 0, 0)),
        ],
        out_specs=[
            pl.BlockSpec((1, seg_rows, 2 * half), lambda j, i: (j, i, 0)),
            pl.BlockSpec((1, seg_rows, 2 * half), lambda j, i: (j, nseg - 1 - i, 0)),
            pl.BlockSpec((1, 4, SSM_ROWS, half), lambda j, i: (j, 0, 0, 0)),
        ],
        out_shape=[f32(nb, rows, 2 * half), f32(nb, rows, 2 * half), f32(nb, 4, SSM_ROWS, half)],
        scratch_shapes=[pltpu.VMEM((seg_rows, 2 * half), _F32), pltpu.VMEM((seg_rows, 2 * half), _F32),
                        pltpu.VMEM((4, SSM_ROWS, half), _F32)],
        compiler_params=_params(("parallel", "arbitrary")),
        name="ssm_scan",
    )(ucat, ucat, wtf, wtb, a, init)


def _ssm_out_kernel(u_ref, xf_ref, xb_ref, tt_ref, rf_ref, rb_ref, y_ref):
    y = jnp.dot(u_ref[...], tt_ref[0], preferred_element_type=_F32)
    y = y + jnp.dot(xf_ref[0].astype(_BF16), rf_ref[0], preferred_element_type=_F32)
    y = y + jnp.dot(xb_ref[0].astype(_BF16), rb_ref[0], preferred_element_type=_F32)
    y_ref[...] = y.astype(_BF16)


def _ssm_out_call(ucat, xf, xb, tt, rf, rb):
    rows, width = ucat.shape
    nb = tt.shape[0]
    kw = width // nb
    sw = xf.shape[2]
    tr = _pick_tile(rows, 512)
    return pl.pallas_call(
        _ssm_out_kernel,
        grid=(nb, rows // tr),
        in_specs=[
            pl.BlockSpec((tr, kw), lambda j, i: (i, j)),
            pl.BlockSpec((1, tr, sw), lambda j, i: (j, i, 0)),
            pl.BlockSpec((1, tr, sw), lambda j, i: (j, i, 0)),
            pl.BlockSpec((1, kw, kw), lambda j, i: (j, 0, 0)),
            pl.BlockSpec((1, sw, kw), lambda j, i: (j, 0, 0)),
            pl.BlockSpec((1, sw, kw), lambda j, i: (j, 0, 0)),
        ],
        out_specs=pl.BlockSpec((tr, kw), lambda j, i: (i, j)),
        out_shape=jax.ShapeDtypeStruct((rows, width), _BF16),
        compiler_params=_params(("parallel", "parallel")),
        name="ssm_out",
    )(ucat, xf, xb, tt, rf, rb)


def _to_chunks(u):
    b, s, _ = u.shape
    nc = s // SSM_CHUNK
    nb = SSM_GROUPS // SSM_BLOCK
    uc = u.reshape(b, nc, SSM_CHUNK, nb, SSM_BLOCK * SSM_GROUP).transpose(1, 0, 3, 2, 4)
    uc = jnp.pad(uc, ((0, 0), (0, SSM_ROWS - b), (0, 0), (0, 0), (0, 0)))
    return uc.reshape(nc * SSM_ROWS, SSM_CHUNK * SSM_WIDTH)


def _from_chunks(y, b):
    rows, _ = y.shape
    nc = rows // SSM_ROWS
    nb = SSM_GROUPS // SSM_BLOCK
    y = y.reshape(nc, SSM_ROWS, nb, SSM_CHUNK, SSM_BLOCK * SSM_GROUP)[:, :b]
    return y.transpose(1, 0, 3, 2, 4).reshape(b, nc * SSM_CHUNK, SSM_WIDTH)


def _mix_kernel(x_ref, mod_ref, o_ref, y_ref, sg_ref, wo_ref, wglu_ref, wout_ref, x1_ref):
    a = jnp.dot(o_ref[0], wo_ref[...], preferred_element_type=_F32)
    yg = jax.nn.gelu(y_ref[0].astype(_F32))
    glu = jnp.dot(yg.astype(_BF16), wglu_ref[...], preferred_element_type=_F32)
    s = glu[:, :D_MODEL] * jax.nn.sigmoid(glu[:, D_MODEL:])
    sg = sg_ref[0].astype(_F32)
    merged = sg[:, :D_MODEL] * a + sg[:, D_MODEL:] * s
    out = jnp.dot(merged.astype(_BF16), wout_ref[...], preferred_element_type=_F32)
    g1 = mod_ref[0][2:3, :]
    x1_ref[0] = x_ref[0] + g1 * out


def _mix_call(x, mod6, o, y, sg, wo, wglu, wout, tm):
    b, s, d = x.shape
    tok = lambda w: pl.BlockSpec((1, tm, w), lambda i, j: (i, j, 0))
    return pl.pallas_call(
        _mix_kernel,
        grid=(b, s // tm),
        in_specs=[tok(d), pl.BlockSpec((1, 6, d), lambda i, j: (i, 0, 0)),
                  tok(o.shape[2]), tok(y.shape[2]), tok(sg.shape[2]),
                  _const_spec(wo.shape), _const_spec(wglu.shape), _const_spec(wout.shape)],
        out_specs=tok(d),
        out_shape=jax.ShapeDtypeStruct((b, s, d), _F32),
        compiler_params=_params(("parallel", "parallel")),
        name="mix",
    )(x, mod6, o, y, sg, wo, wglu, wout)


def _ffn_kernel(x_ref, prev_ref, next_ref, mod_ref, g2_ref, wup_ref, cw_ref, cb_ref, wdn_ref,
                out_ref, acc_ref, *, n_chunks):
    j = pl.program_id(1)
    nj = pl.num_programs(1)
    tm = x_ref.shape[1]
    mod = mod_ref[0]
    sh2 = mod[3:4, :]
    sc2 = mod[4:5, :]
    g2 = mod[5:6, :]
    gain = g2_ref[...]

    def prenorm(v):
        return ((_rms(v, D_MODEL) * gain) * (1.0 + sc2) + sh2).astype(_BF16)

    x = x_ref[0]
    h = prenorm(jnp.concatenate([x, prev_ref[0], next_ref[0]], axis=0))
    has_prev = (j > 0).astype(_F32)
    has_next = (j < nj - 1).astype(_F32)
    row = lax.broadcasted_iota(jnp.int32, (8, 2 * FFN_CHUNK), 0)

    def up(ci):
        co = ci * 2 * FFN_CHUNK
        return jnp.dot(h, wup_ref[:, co:co + 2 * FFN_CHUNK], preferred_element_type=_F32)

    def gated(ci, pe):
        co = ci * 2 * FFN_CHUNK
        p = pe[:tm]
        before = pltpu.roll(p, 1, 0)
        after = pltpu.roll(p, tm - 1, 0)
        first = jnp.where(row == 0, pe[tm + 7:tm + 8] * has_prev, before[:8])
        last = jnp.where(row == 7, pe[tm + 8:tm + 9] * has_next, after[tm - 8:])
        before = jnp.concatenate([first, before[8:]], axis=0)
        after = jnp.concatenate([after[:tm - 8], last], axis=0)
        cw = cw_ref[:, co:co + 2 * FFN_CHUNK]
        uc = before * cw[0:1, :] + p * cw[1:2, :] + after * cw[2:3, :] + cb_ref[:, co:co + 2 * FFN_CHUNK]
        val = uc[:, :FFN_CHUNK]
        gate = uc[:, FFN_CHUNK:]
        return (gate * jax.nn.sigmoid(gate) * val).astype(_BF16)

    pe = up(0)
    for ci in range(n_chunks):
        nxt = up(ci + 1) if ci + 1 < n_chunks else None
        act = gated(ci, pe)
        dn = jnp.dot(act, wdn_ref[ci * FFN_CHUNK:(ci + 1) * FFN_CHUNK, :], preferred_element_type=_F32)
        if ci == 0:
            acc_ref[...] = dn
        else:
            acc_ref[...] += dn
        pe = nxt
    out_ref[0] = x + g2 * acc_ref[...]


def _ffn_call(x1, mod6, g2, wup, cw, cb, wdn, tm):
    b, s, d = x1.shape
    n_chunks = wdn.shape[0] // FFN_CHUNK
    hb = tm // 8
    last = s // 8 - 1
    return pl.pallas_call(
        functools.partial(_ffn_kernel, n_chunks=n_chunks),
        grid=(b, s // tm),
        in_specs=[
            pl.BlockSpec((1, tm, d), lambda i, j: (i, j, 0)),
            pl.BlockSpec((1, 8, d), lambda i, j: (i, jnp.maximum(j * hb - 1, 0), 0)),
            pl.BlockSpec((1, 8, d), lambda i, j: (i, jnp.minimum((j + 1) * hb, last), 0)),
            pl.BlockSpec((1, 6, d), lambda i, j: (i, 0, 0)),
            _const_spec(g2.shape), _const_spec(wup.shape), _const_spec(cw.shape),
            _const_spec(cb.shape), _const_spec(wdn.shape),
        ],
        out_specs=pl.BlockSpec((1, tm, d), lambda i, j: (i, j, 0)),
        out_shape=jax.ShapeDtypeStruct((b, s, d), _F32),
        scratch_shapes=[pltpu.VMEM((tm, d), _F32)],
        compiler_params=_params(("parallel", "arbitrary")),
        name="ffn",
    )(x1, x1, x1, mod6, g2, wup, cw, cb, wdn)


def _rope_tables(s):
    rows = s // GRID_W
    row = jnp.repeat(jnp.arange(rows), GRID_W)
    col = jnp.tile(jnp.arange(GRID_W), rows)
    pairs = QK_ROPE // 4
    freqs = ROPE_THETA ** (-jnp.arange(pairs, dtype=_F32) / pairs)
    ang = jnp.concatenate([row[:, None] * freqs, col[:, None] * freqs], axis=-1)
    cos, sin = jnp.cos(ang), jnp.sin(ang)
    z = lambda w: jnp.zeros((s, w), _F32)
    ctab = jnp.concatenate([jnp.ones((s, QK_NOPE), _F32), cos, cos, z(HEAD_PAD - QK_DIM)], axis=-1)
    stab = jnp.concatenate([z(QK_NOPE), -sin, sin, z(HEAD_PAD - QK_DIM)], axis=-1)
    return ctab, stab


def _identity_tables(s):
    ctab = jnp.concatenate([jnp.ones((s, QK_DIM), _F32), jnp.zeros((s, HEAD_PAD - QK_DIM), _F32)], axis=-1)
    return ctab, jnp.zeros((s, HEAD_PAD), _F32)


def _swap_rope_halves(w):
    half = QK_ROPE // 2
    return jnp.concatenate([jnp.zeros_like(w[..., :QK_NOPE]), w[..., QK_NOPE + half:QK_DIM],
                            w[..., QK_NOPE:QK_NOPE + half]], axis=-1)


def _pick_tile(n, pref):
    t = min(n, pref)
    while n % t:
        t //= 2
    return t


def kernel(x, c, ctx, c_ctx, w_mod, b_mod, norm1_g, norm2_g, w_in, q_a_g, w_uq, kv_a_g, w_ukv, q_norm_g, k_norm_g, w_o_attn, lam_re_f, lam_im_f, log_dt_f, c_re_f, c_im_f, lam_re_b, lam_im_b, log_dt_b, c_re_b, c_im_b, b_re, b_im, d_skip, w_glu, w_out, w_up, conv_w, conv_b, w_down):
    b, s, d = x.shape
    n_ctx = ctx.shape[1]
    depth = w_mod.shape[0]
    assert depth == 1, "context update between layers is not implemented"
    l = 0

    wi = w_in[l]
    o1, o2, o3, o4 = Q_LORA, Q_LORA + KV_LORA, Q_LORA + KV_LORA + QK_ROPE, Q_LORA + KV_LORA + QK_ROPE + SSM_WIDTH
    zc = lambda w: jnp.zeros((d, w), wi.dtype)
    half = QK_ROPE // 2
    win = jnp.concatenate([wi[:, :o1], wi[:, o1:o2], wi[:, o3:o4], wi[:, o4:],
                           zc(QK_NOPE), wi[:, o2:o3], zc(HEAD_PAD - QK_DIM),
                           zc(QK_NOPE), wi[:, o2 + half:o3], wi[:, o2:o2 + half], zc(HEAD_PAD - QK_DIM)],
                          axis=-1).astype(_BF16)
    pad_head = lambda w: jnp.pad(w, [(0, 0)] * (w.ndim - 1) + [(0, HEAD_PAD - QK_DIM)])
    wq3 = w_uq[l].reshape(Q_LORA, N_HEADS, QK_DIM)
    wuq = jnp.concatenate([pad_head(wq3).reshape(Q_LORA, N_HEADS * HEAD_PAD),
                           pad_head(_swap_rope_halves(wq3)).reshape(Q_LORA, N_HEADS * HEAD_PAD)],
                          axis=-1).astype(_BF16)
    wkv = w_ukv[l].reshape(KV_LORA, N_HEADS, QK_NOPE + V_DIM)
    wk = jnp.pad(wkv[:, :, :QK_NOPE], ((0, 0), (0, 0), (0, HEAD_PAD - QK_NOPE))).reshape(KV_LORA, N_HEADS * HEAD_PAD)
    wv = wkv[:, :, QK_NOPE:].reshape(KV_LORA, N_HEADS * V_DIM)
    wukv = jnp.concatenate([wk, wv], axis=-1).astype(_BF16)
    padg = lambda g: jnp.stack([pad_head(g), pad_head(_swap_rope_halves(g))], axis=0)
    qng, kng = padg(q_norm_g[l]), padg(k_norm_g[l])
    g1 = norm1_g[l].reshape(1, d)
    g2 = norm2_g[l].reshape(1, d)
    qag = q_a_g[l].reshape(1, Q_LORA)
    kvag = kv_a_g[l].reshape(1, KV_LORA)
    nch = FFN_HIDDEN // FFN_CHUNK
    pair = lambda w: jnp.concatenate(
        [w[..., :FFN_HIDDEN].reshape(w.shape[:-1] + (nch, 1, FFN_CHUNK)),
         w[..., FFN_HIDDEN:].reshape(w.shape[:-1] + (nch, 1, FFN_CHUNK))], axis=-2
    ).reshape(w.shape[:-1] + (2 * FFN_HIDDEN,))
    wup = pair(w_up[l]).astype(_BF16)
    cw = pair(conv_w[l])
    cb = pair(conv_b[l].reshape(1, 2 * FFN_HIDDEN))
    wdn = w_down[l].astype(_BF16)
    wo = w_o_attn[l].astype(_BF16)
    wglu = w_glu[l].astype(_BF16)
    wout = w_out[l].astype(_BF16)

    cc = jnp.concatenate([c, c_ctx[None, :], jnp.zeros((8 - b - 1, d), c.dtype)], axis=0)
    mod = _mod_call(cc, w_mod[l], b_mod[l])
    mod_lat = mod[:b].reshape(b, 6, d)
    mod_ctx = jnp.broadcast_to(mod[b].reshape(1, 6, d), (b, 6, d))

    tm = _pick_tile(s, 512)
    tmc = _pick_tile(n_ctx, 512)
    shared = (g1, win, qag, wuq, kvag, wukv, qng, kng)
    qt, k, vt, u, sg = _inproj_call(x, mod_lat, *shared, *_rope_tables(s), tm)
    _, k_c, vt_c, u_c, _ = _inproj_call(ctx, mod_ctx, *shared, *_identity_tables(n_ctx), tmc)

    wtf, wtb, tt, rf, rb, a = _ssm_tables(lam_re_f[l], lam_im_f[l], log_dt_f[l], c_re_f[l], c_im_f[l],
                                          lam_re_b[l], lam_im_b[l], log_dt_b[l], c_re_b[l], c_im_b[l],
                                          b_re[l], b_im[l], d_skip[l])
    zero_init = jnp.zeros((SSM_GROUPS // SSM_BLOCK, 4, SSM_ROWS, SSM_BLOCK * SSM_STATE), _F32)
    _, _, fin_c = _ssm_scan_call(_to_chunks(u_c), wtf, wtb, a, zero_init)
    ucat = _to_chunks(u)
    xf, xb, _ = _ssm_scan_call(ucat, wtf, wtb, a, fin_c)
    y = _from_chunks(_ssm_out_call(ucat, xf, xb, tt, rf, rb), b)

    k_all = jnp.concatenate([k, k_c], axis=2)
    vt_all = jnp.concatenate([vt, vt_c], axis=-1)
    tq = _pick_tile(s, 512)
    tk = _pick_tile(s + n_ctx, 256)
    o = _attn_call(qt, k_all, vt_all, tq, tk)

    x1 = _mix_call(x, mod_lat, o, y, sg, wo, wglu, wout, tm)
    return _ffn_call(x1, mod_lat, g2, wup, cw, cb, wdn, tm)
```

```python
import functools
import math

import jax
import jax.numpy as jnp
from jax import lax
from jax.experimental import pallas as pl
from jax.experimental.pallas import tpu as pltpu

D_MODEL = 1024
GRID_W = 64
N_HEADS = 8
QK_NOPE = 64
QK_ROPE = 32
QK_DIM = QK_NOPE + QK_ROPE
V_DIM = 64
Q_LORA = 384
KV_LORA = 256
ROPE_THETA = 10000.0
SSM_WIDTH = 512
SSM_GROUP = 16
SSM_GROUPS = SSM_WIDTH // SSM_GROUP
SSM_STATE = 64
FFN_HIDDEN = 2816
EPS = 1e-6

LANES = 128
HEAD_PAD = LANES
ONES_ROWS = 16
INPROJ_SUB = 128
ATTN_TK = 256
ATTN_HEADS = 2
SSM_CHUNK = 16
SSM_ROWS = 8
SSM_BLOCK = 8
SSM_SEG_CHUNKS = 64
FFN_CHUNK = 256
VMEM_LIMIT = 56 * 1024 * 1024

_HI = lax.Precision.HIGHEST
_F32 = jnp.float32
_BF16 = jnp.bfloat16


def _params(sem):
    return pltpu.CompilerParams(dimension_semantics=sem, vmem_limit_bytes=VMEM_LIMIT)


def _const_spec(shape):
    nd = len(shape)
    return pl.BlockSpec(shape, lambda *_: (0,) * nd)


def _rms(v, width):
    return v * lax.rsqrt(jnp.sum(v * v, axis=-1, keepdims=True) * (1.0 / width) + EPS)


def _mod_kernel(c_ref, w_ref, b_ref, o_ref):
    c = c_ref[...]
    s = c * jax.nn.sigmoid(c)
    o_ref[...] = jnp.dot(s, w_ref[...], precision=_HI, preferred_element_type=_F32) + b_ref[...]


def _mod_call(cc, w_mod, b_mod):
    rows, d = cc.shape
    n = w_mod.shape[1]
    tn = 1024
    return pl.pallas_call(
        _mod_kernel,
        grid=(n // tn,),
        in_specs=[
            pl.BlockSpec((rows, d), lambda j: (0, 0)),
            pl.BlockSpec((d, tn), lambda j: (0, j)),
            pl.BlockSpec((1, tn), lambda j: (0, j)),
        ],
        out_specs=pl.BlockSpec((rows, tn), lambda j: (0, j)),
        out_shape=jax.ShapeDtypeStruct((rows, n), _F32),
        compiler_params=_params(("arbitrary",)),
        name="mod",
    )(cc, w_mod, b_mod.reshape(1, n))


_C_CQ = 0
_C_CKV = _C_CQ + Q_LORA
_C_U = _C_CKV + KV_LORA
_C_GL = _C_U + SSM_WIDTH
_C_KR = _C_GL + 2 * D_MODEL
_C_KRP = _C_KR + LANES
_IN_W = _C_KRP + LANES
_QW = N_HEADS * HEAD_PAD


def _inproj_kernel(x_ref, mod_ref, g1_ref, win_ref, qag_ref, wuq_ref, kvag_ref, wukv_ref,
                   qng_ref, kng_ref, ctab_ref, stab_ref,
                   qt_ref, k_ref, vt_ref, u_ref, sg_ref, *, sub):
    tm = x_ref.shape[1]
    mod = mod_ref[0]
    sh1 = mod[0:1, :]
    sc1 = 1.0 + mod[1:2, :]
    g1 = g1_ref[...]
    qg = qng_ref[...] * (QK_DIM ** -0.5 * math.log2(math.e))
    kg = kng_ref[...]

    def project(i):
        x = x_ref[0, i * sub:(i + 1) * sub, :]
        h = (_rms(x, D_MODEL) * g1) * sc1 + sh1
        return jnp.dot(h.astype(_BF16), win_ref[...], preferred_element_type=_F32)

    def expand(i, proj):
        rows = slice(i * sub, (i + 1) * sub)
        u_ref[0, rows, :] = proj[:, _C_U:_C_GL].astype(_BF16)
        sg_ref[0, rows, :] = jax.nn.sigmoid(proj[:, _C_GL:_C_KR]).astype(_BF16)
        cq = _rms(proj[:, _C_CQ:_C_CKV], Q_LORA) * qag_ref[...]
        qall = jnp.dot(cq.astype(_BF16), wuq_ref[...], preferred_element_type=_F32)
        ckv = _rms(proj[:, _C_CKV:_C_U], KV_LORA) * kvag_ref[...]
        kvall = jnp.dot(ckv.astype(_BF16), wukv_ref[...], preferred_element_type=_F32)
        return qall, kvall, proj[:, _C_KR:_C_KRP], proj[:, _C_KRP:_IN_W]

    def heads(i, qall, kvall, kr, krp):
        rows = slice(i * sub, (i + 1) * sub)
        ctab = ctab_ref[rows, :]
        stab = stab_ref[rows, :]
        qc = qg[0:1] * ctab
        qs = qg[1:2] * stab
        kc = kg[0:1] * ctab
        k_rot = krp * (kg[1:2] * stab)
        for hd in range(N_HEADS):
            lo = hd * HEAD_PAD
            qh = qall[:, lo:lo + HEAD_PAD]
            nq = lax.rsqrt(jnp.sum(qh * qh, axis=-1, keepdims=True) * (1.0 / QK_DIM) + EPS)
            qr = (qh * qc + qall[:, _QW + lo:_QW + lo + HEAD_PAD] * qs) * nq
            qt_ref[0, hd, :, rows] = qr.T.astype(_BF16)
            kh = kvall[:, lo:lo + HEAD_PAD] + kr
            nk = lax.rsqrt(jnp.sum(kh * kh, axis=-1, keepdims=True) * (1.0 / QK_DIM) + EPS)
            k_ref[0, hd, rows, :] = ((kh * kc + k_rot) * nk).astype(_BF16)
        for hp in range(N_HEADS // 2):
            lo = _QW + hp * LANES
            vt = kvall[:, lo:lo + LANES].T.astype(_BF16)
            vt_ref[0, 2 * hp, :, rows] = vt[:V_DIM]
            vt_ref[0, 2 * hp + 1, :, rows] = vt[V_DIM:]

    n_sub = tm // sub
    proj = project(0)
    for i in range(n_sub):
        nxt = project(i + 1) if i + 1 < n_sub else None
        heads(i, *expand(i, proj))
        proj = nxt


def _inproj_call(x, mod6, g1, win, qag, wuq, kvag, wukv, qng, kng, ctab, stab, tm):
    b, s, d = x.shape
    grid = (b, s // tm)
    sub = min(tm, INPROJ_SUB)
    tok = lambda w: pl.BlockSpec((1, tm, w), lambda i, j: (i, j, 0))
    tab = pl.BlockSpec((tm, LANES), lambda i, j: (j, 0))
    return pl.pallas_call(
        functools.partial(_inproj_kernel, sub=sub),
        grid=grid,
        in_specs=[
            tok(d),
            pl.BlockSpec((1, 6, d), lambda i, j: (i, 0, 0)),
            _const_spec(g1.shape), _const_spec(win.shape), _const_spec(qag.shape),
            _const_spec(wuq.shape), _const_spec(kvag.shape), _const_spec(wukv.shape),
            _const_spec(qng.shape), _const_spec(kng.shape),
            tab, tab,
        ],
        out_specs=[
            pl.BlockSpec((1, N_HEADS, HEAD_PAD, tm), lambda i, j: (i, 0, 0, j)),
            pl.BlockSpec((1, N_HEADS, tm, HEAD_PAD), lambda i, j: (i, 0, j, 0)),
            pl.BlockSpec((1, N_HEADS, V_DIM, tm), lambda i, j: (i, 0, 0, j)),
            tok(SSM_WIDTH), tok(2 * D_MODEL),
        ],
        out_shape=[
            jax.ShapeDtypeStruct((b, N_HEADS, HEAD_PAD, s), _BF16),
            jax.ShapeDtypeStruct((b, N_HEADS, s, HEAD_PAD), _BF16),
            jax.ShapeDtypeStruct((b, N_HEADS, V_DIM, s), _BF16),
            jax.ShapeDtypeStruct((b, s, SSM_WIDTH), _BF16),
            jax.ShapeDtypeStruct((b, s, 2 * D_MODEL), _BF16),
        ],
        compiler_params=_params(("parallel", "parallel")),
        name="inproj",
    )(x, mod6, g1, win, qag, wuq, kvag, wukv, qng, kng, ctab, stab)


def _attn_kernel(qt_ref, k_ref, vt_ref, kc_ref, vtc_ref, o_ref, s_a, s_b, c_a, c_b, p_a, p_b,
                 acc_ref, m_ref, al_ref, *, tiles):
    heads = range(ATTN_HEADS)
    n_kt = len(tiles)
    even = (s_a, c_a, p_a)
    odd = (s_b, c_b, p_b)

    def scores(t, j, dst):
        ctx, off, size = tiles[t]
        keys = (kc_ref if ctx else k_ref)[0, j, off:off + size, :]
        s = jnp.dot(keys, qt_ref[0, j], preferred_element_type=_F32)
        dst[0][j, :size] = s
        dst[1][j] = jnp.max(s, axis=0, keepdims=True)

    def weighted(t, j, p_in):
        ctx, off, size = tiles[t]
        vt = (vtc_ref if ctx else vt_ref)[0, j, :, off:off + size]
        lhs = jnp.concatenate([vt, jnp.ones((ONES_ROWS, size), _BF16)], axis=0)
        pv = jnp.dot(lhs, p_in[j, :size], preferred_element_type=_F32)
        acc_ref[j] = acc_ref[j] * al_ref[j] + pv

    def softmax(t, j, cur):
        size = tiles[t][2]
        m = m_ref[j]
        m_new = jnp.maximum(m, cur[1][j])
        al_ref[j] = jnp.exp2(m - m_new)
        m_ref[j] = m_new
        cur[2][j, :size] = jnp.exp2((cur[0][j, :size] - m_new).astype(_BF16))

    def step(t, cur, nxt):
        if t > 0:
            for j in heads:
                weighted(t - 1, j, nxt[2])
        if t + 1 < n_kt:
            for j in heads:
                scores(t + 1, j, nxt)
        for j in heads:
            softmax(t, j, cur)

    for j in heads:
        m_ref[j] = jnp.full(m_ref.shape[1:], -jnp.inf, _F32)
        al_ref[j] = jnp.ones(al_ref.shape[1:], _F32)
        acc_ref[j] = jnp.zeros(acc_ref.shape[1:], _F32)
        scores(0, j, even)

    for t in range(n_kt):
        if t % 2 == 0:
            step(t, even, odd)
        else:
            step(t, odd, even)
    p_last = p_a if n_kt % 2 else p_b
    outs = []
    for j in heads:
        weighted(n_kt - 1, j, p_last)
        acc = acc_ref[j]
        outs.append(acc[:V_DIM] / acc[V_DIM:V_DIM + 1])
    o_ref[0] = jnp.concatenate(outs, axis=0).T.astype(_BF16)


def _attn_call(qt, k, vt, k_c, vt_c, tq, tk):
    b, h, _, s = qt.shape
    n_ctx = k_c.shape[2]
    tkc = min(tk, n_ctx)
    tiles = tuple((False, o, tk) for o in range(0, s, tk)) + tuple((True, o, tkc) for o in range(0, n_ctx, tkc))
    nh = ATTN_HEADS
    group = lambda rows, cols: pl.BlockSpec((1, nh, rows, cols), lambda i, p, j: (i, p, 0, 0))
    return pl.pallas_call(
        functools.partial(_attn_kernel, tiles=tiles),
        grid=(b, h // nh, s // tq),
        in_specs=[
            pl.BlockSpec((1, nh, HEAD_PAD, tq), lambda i, p, j: (i, p, 0, j)),
            group(s, HEAD_PAD), group(V_DIM, s), group(n_ctx, HEAD_PAD), group(V_DIM, n_ctx),
        ],
        out_specs=pl.BlockSpec((1, tq, nh * V_DIM), lambda i, p, j: (i, j, p)),
        out_shape=jax.ShapeDtypeStruct((b, s, h * V_DIM), _BF16),
        scratch_shapes=[
            pltpu.VMEM((nh, tk, tq), _F32), pltpu.VMEM((nh, tk, tq), _F32),
            pltpu.VMEM((nh, 1, tq), _F32), pltpu.VMEM((nh, 1, tq), _F32),
            pltpu.VMEM((nh, tk, tq), _BF16), pltpu.VMEM((nh, tk, tq), _BF16),
            pltpu.VMEM((nh, V_DIM + ONES_ROWS, tq), _F32),
            pltpu.VMEM((nh, 1, tq), _F32), pltpu.VMEM((nh, 1, tq), _F32),
        ],
        compiler_params=_params(("parallel", "parallel", "arbitrary")),
        name="attn",
    )(qt, k, vt, k_c, vt_c)


_SB = SSM_BLOCK * SSM_GROUP
_SN = SSM_BLOCK * SSM_STATE


def _cmul(ar, ai, br, bi):
    return ar * br - ai * bi, ar * bi + ai * br


def _ssm_powers(lre, lim, ldt):
    dt = jnp.exp(ldt)
    mag = jnp.exp(lre * dt)
    ar = mag * jnp.cos(lim * dt)
    ai = mag * jnp.sin(lim * dt)
    nr = ar - 1.0
    den = lre * lre + lim * lim
    cfr = (nr * lre + ai * lim) / den
    cfi = (ai * lre - nr * lim) / den
    pr = [jnp.ones_like(ar)]
    pi = [jnp.zeros_like(ar)]
    for _ in range(SSM_CHUNK):
        r, i = _cmul(pr[-1], pi[-1], ar, ai)
        pr.append(r)
        pi.append(i)
    return pr, pi, cfr, cfi


def _same_group():
    row = lax.broadcasted_iota(jnp.int32, (_SB, _SN), 0) // SSM_GROUP
    col = lax.broadcasted_iota(jnp.int32, (_SB, _SN), 1) // SSM_STATE
    return row == col


def _ssmtab_kernel(lre_ref, lim_ref, ldt_ref, mre_ref, mim_ref, dsk_ref, *outs, what):
    same = _same_group()
    zero = jnp.zeros((_SB, _SN), _F32)
    pw = [_ssm_powers(lre_ref[d, 0], lim_ref[d, 0], ldt_ref[d, 0]) for d in range(2)]

    def input_matrix(d, bre, bim):
        _, _, cfr, cfi = pw[d]
        r, i = _cmul(bre, bim, cfr, cfi)
        return jnp.where(same, r, zero), jnp.where(same, i, zero)

    if what == "w":
        (wt_ref,) = outs
        for d in range(2):
            pr, pi = pw[d][:2]
            bbr, bbi = input_matrix(d, mre_ref[0], mim_ref[0])
            for s in range(SSM_CHUNK):
                k = s if d else SSM_CHUNK - 1 - s
                r, i = _cmul(bbr, bbi, pr[k], pi[k])
                wt_ref[d, 0, s * _SB:(s + 1) * _SB, :_SN] = r.astype(_BF16)
                wt_ref[d, 0, s * _SB:(s + 1) * _SB, _SN:] = i.astype(_BF16)
    elif what == "r":
        r_ref, a_ref = outs
        for d in range(2):
            pr, pi = pw[d][:2]
            cr = jnp.where(same, mre_ref[d, 0], zero)
            ci = jnp.where(same, mim_ref[d, 0], zero)
            for t in range(SSM_CHUNK):
                k = SSM_CHUNK - t if d else t + 1
                r, i = _cmul(cr, ci, pr[k], pi[k])
                r_ref[d, 0, :_SN, t * _SB:(t + 1) * _SB] = r.T.astype(_BF16)
                r_ref[d, 0, _SN:, t * _SB:(t + 1) * _SB] = (-i).T.astype(_BF16)
            a_ref[0, 2 * d:2 * d + 1, :] = pr[SSM_CHUNK]
            a_ref[0, 2 * d + 1:2 * d + 2, :] = pi[SSM_CHUNK]
    else:
        cre_ref, cim_ref, tt_ref, lag_ref = outs
        dn = (((1,), (1,)), ((), ()))
        for d in range(2):
            pr, pi = pw[d][:2]
            bbr, bbi = input_matrix(d, mre_ref[0], mim_ref[0])
            cr = jnp.where(same, cre_ref[d, 0], zero)
            ci = jnp.where(same, cim_ref[d, 0], zero)
            for k in range(SSM_CHUNK):
                r, i = _cmul(bbr, bbi, pr[k], pi[k])
                kk = (lax.dot_general(r, cr, dn, precision=_HI, preferred_element_type=_F32)
                      - lax.dot_general(i, ci, dn, precision=_HI, preferred_element_type=_F32))
                lag_ref[d * SSM_CHUNK + k] = kk
        row = lax.broadcasted_iota(jnp.int32, (_SB, _SB), 0)
        col = lax.broadcasted_iota(jnp.int32, (_SB, _SB), 1)
        lag_ref[0] = lag_ref[0] + lag_ref[SSM_CHUNK] + jnp.where(row == col, dsk_ref[0], 0.0)
        for s in range(SSM_CHUNK):
            for t in range(SSM_CHUNK):
                k = t - s if t >= s else SSM_CHUNK + s - t
                tt_ref[0, s * _SB:(s + 1) * _SB, t * _SB:(t + 1) * _SB] = lag_ref[k].astype(_BF16)


def _ssmtab_call(what, lre, lim, ldt, mre, mim, dsk, cre=None, cim=None):
    nb = mre.shape[-3]
    row = pl.BlockSpec((2, 1, 1, _SN), lambda j: (0, j, 0, 0))
    mat1 = pl.BlockSpec((1, _SB, _SN), lambda j: (j, 0, 0))
    mat2 = pl.BlockSpec((2, 1, _SB, _SN), lambda j: (0, j, 0, 0))
    dspec = pl.BlockSpec((1, 1, _SB), lambda j: (j, 0, 0))
    l = SSM_CHUNK
    scratch = []
    if what == "w":
        in_specs, args = [row, row, row, mat1, mat1, dspec], (lre, lim, ldt, mre, mim, dsk)
        out_specs = [pl.BlockSpec((2, 1, l * _SB, 2 * _SN), lambda j: (0, j, 0, 0))]
        out_shape = [jax.ShapeDtypeStruct((2, nb, l * _SB, 2 * _SN), _BF16)]
    elif what == "r":
        in_specs, args = [row, row, row, mat2, mat2, dspec], (lre, lim, ldt, mre, mim, dsk)
        out_specs = [pl.BlockSpec((2, 1, 2 * _SN, l * _SB), lambda j: (0, j, 0, 0)),
                     pl.BlockSpec((1, 4, _SN), lambda j: (j, 0, 0))]
        out_shape = [jax.ShapeDtypeStruct((2, nb, 2 * _SN, l * _SB), _BF16),
                     jax.ShapeDtypeStruct((nb, 4, _SN), _F32)]
    else:
        in_specs, args = [row, row, row, mat1, mat1, dspec, mat2, mat2], (lre, lim, ldt, mre, mim, dsk, cre, cim)
        out_specs = [pl.BlockSpec((1, l * _SB, l * _SB), lambda j: (j, 0, 0))]
        out_shape = [jax.ShapeDtypeStruct((nb, l * _SB, l * _SB), _BF16)]
        scratch = [pltpu.VMEM((2 * l, _SB, _SB), _F32)]
    return pl.pallas_call(
        functools.partial(_ssmtab_kernel, what=what),
        grid=(nb,),
        in_specs=in_specs,
        out_specs=out_specs,
        out_shape=out_shape,
        scratch_shapes=scratch,
        compiler_params=_params(("parallel",)),
        name="ssmtab_" + what,
    )(*args)


def _ssm_tables(lam_re_f, lam_im_f, log_dt_f, c_re_f, c_im_f,
                lam_re_b, lam_im_b, log_dt_b, c_re_b, c_im_b, b_re, b_im, d_skip):
    g, n, c, gb = SSM_GROUPS, SSM_STATE, SSM_GROUP, SSM_BLOCK
    nb = g // gb
    rows = lambda f, b: jnp.stack([f, b]).reshape(2, nb, 1, gb * n)
    lre = rows(lam_re_f, lam_re_b)
    lim = rows(lam_im_f, lam_im_b)
    ldt = rows(jnp.broadcast_to(log_dt_f[:, None], (g, n)), jnp.broadcast_to(log_dt_b[:, None], (g, n)))
    tiled = lambda m: jnp.tile(m.reshape(nb, gb * c, n), (1, 1, gb))
    btre = tiled(b_re.transpose(0, 2, 1))
    btim = tiled(b_im.transpose(0, 2, 1))
    cre = jnp.stack([tiled(c_re_f), tiled(c_re_b)])
    cim = jnp.stack([tiled(c_im_f), tiled(c_im_b)])
    dsk = d_skip.reshape(nb, 1, gb * c)
    (wt,) = _ssmtab_call("w", lre, lim, ldt, btre, btim, dsk)
    r, a = _ssmtab_call("r", lre, lim, ldt, cre, cim, dsk)
    (tt,) = _ssmtab_call("t", lre, lim, ldt, btre, btim, dsk, cre, cim)
    return wt[0], wt[1], tt, r[0], r[1], a


def _ssm_scan_kernel(uf_ref, ub_ref, wtf_ref, wtb_ref, a_ref, init_ref, xf_ref, xb_ref, fin_ref,
                     ef, eb, st, *, segc):
    i = pl.program_id(1)
    half = SSM_BLOCK * SSM_STATE

    @pl.when(i == 0)
    def _():
        st[...] = init_ref[0]

    ef[...] = jnp.dot(uf_ref[...], wtf_ref[0], preferred_element_type=_F32)
    eb[...] = jnp.dot(ub_ref[...], wtb_ref[0], preferred_element_type=_F32)
    a = a_ref[0]
    afr, afi, abr, abi = (jnp.broadcast_to(a[k:k + 1, :], (SSM_ROWS, half)) for k in range(4))

    def step(c, carry):
        fr, fi, br, bi = carry
        rf = pl.ds(pl.multiple_of(c * SSM_ROWS, SSM_ROWS), SSM_ROWS)
        rb = pl.ds(pl.multiple_of((segc - 1 - c) * SSM_ROWS, SSM_ROWS), SSM_ROWS)
        xf_ref[0, rf, :] = jnp.concatenate([fr, fi], axis=-1)
        xb_ref[0, rb, :] = jnp.concatenate([br, bi], axis=-1)
        e_f = ef[rf, :]
        e_b = eb[rb, :]
        nfr = afr * fr - afi * fi + e_f[:, :half]
        nfi = afr * fi + afi * fr + e_f[:, half:]
        nbr = abr * br - abi * bi + e_b[:, :half]
        nbi = abr * bi + abi * br + e_b[:, half:]
        return nfr, nfi, nbr, nbi

    fin = lax.fori_loop(0, segc, step, tuple(st[k] for k in range(4)))
    for k in range(4):
        st[k] = fin[k]
        fin_ref[0, k] = fin[k]


def _ssm_scan_call(ucat, wtf, wtb, a, init):
    rows, width = ucat.shape
    nb = wtf.shape[0]
    kw = width // nb
    half = SSM_BLOCK * SSM_STATE
    nc = rows // SSM_ROWS
    segc = min(nc, SSM_SEG_CHUNKS)
    nseg = nc // segc
    seg_rows = segc * SSM_ROWS
    f32 = lambda *shape: jax.ShapeDtypeStruct(shape, _F32)
    return pl.pallas_call(
        functools.partial(_ssm_scan_kernel, segc=segc),
        grid=(nb, nseg),
        in_specs=[
            pl.BlockSpec((seg_rows, kw), lambda j, i: (i, j)),
            pl.BlockSpec((seg_rows, kw), lambda j, i: (nseg - 1 - i, j)),
            pl.BlockSpec((1, kw, 2 * half), lambda j, i: (j, 0, 0)),
            pl.BlockSpec((1, kw, 2 * half), lambda j, i: (j, 0, 0)),
            pl.BlockSpec((1, 4, half), lambda j, i: (j, 0, 0)),
            pl.BlockSpec((1, 4, SSM_ROWS, half), lambda j, i: (j, 0, 0, 0)),
        ],
        out_specs=[
            pl.BlockSpec((1, seg_rows, 2 * half), lambda j, i: (j, i, 0)),
            pl.BlockSpec((1, seg_rows, 2 * half), lambda j, i: (j, nseg - 1 - i, 0)),
            pl.BlockSpec((1, 4, SSM_ROWS, half), lambda j, i: (j, 0, 0, 0)),
        ],
        out_shape=[f32(nb, rows, 2 * half), f32(nb, rows, 2 * half), f32(nb, 4, SSM_ROWS, half)],
        scratch_shapes=[pltpu.VMEM((seg_rows, 2 * half), _F32), pltpu.VMEM((seg_rows, 2 * half), _F32),
                        pltpu.VMEM((4, SSM_ROWS, half), _F32)],
        compiler_params=_params(("parallel", "arbitrary")),
        name="ssm_scan",
    )(ucat, ucat, wtf, wtb, a, init)


def _ssm_out_kernel(u_ref, xf_ref, xb_ref, tt_ref, rf_ref, rb_ref, y_ref):
    y = jnp.dot(u_ref[...], tt_ref[0], preferred_element_type=_F32)
    y = y + jnp.dot(xf_ref[0].astype(_BF16), rf_ref[0], preferred_element_type=_F32)
    y = y + jnp.dot(xb_ref[0].astype(_BF16), rb_ref[0], preferred_element_type=_F32)
    y_ref[...] = y.astype(_BF16)


def _ssm_out_call(ucat, xf, xb, tt, rf, rb):
    rows, width = ucat.shape
    nb = tt.shape[0]
    kw = width // nb
    sw = xf.shape[2]
    tr = _pick_tile(rows, 512)
    return pl.pallas_call(
        _ssm_out_kernel,
        grid=(nb, rows // tr),
        in_specs=[
            pl.BlockSpec((tr, kw), lambda j, i: (i, j)),
            pl.BlockSpec((1, tr, sw), lambda j, i: (j, i, 0)),
            pl.BlockSpec((1, tr, sw), lambda j, i: (j, i, 0)),
            pl.BlockSpec((1, kw, kw), lambda j, i: (j, 0, 0)),
            pl.BlockSpec((1, sw, kw), lambda j, i: (j, 0, 0)),
            pl.BlockSpec((1, sw, kw), lambda j, i: (j, 0, 0)),
        ],
        out_specs=pl.BlockSpec((tr, kw), lambda j, i: (i, j)),
        out_shape=jax.ShapeDtypeStruct((rows, width), _BF16),
        compiler_params=_params(("parallel", "parallel")),
        name="ssm_out",
    )(ucat, xf, xb, tt, rf, rb)


def _to_chunks(u):
    b, s, _ = u.shape
    nc = s // SSM_CHUNK
    nb = SSM_GROUPS // SSM_BLOCK
    uc = u.reshape(b, nc, SSM_CHUNK, nb, SSM_BLOCK * SSM_GROUP).transpose(1, 0, 3, 2, 4)
    uc = jnp.pad(uc, ((0, 0), (0, SSM_ROWS - b), (0, 0), (0, 0), (0, 0)))
    return uc.reshape(nc * SSM_ROWS, SSM_CHUNK * SSM_WIDTH)


def _from_chunks(y, b):
    rows, _ = y.shape
    nc = rows // SSM_ROWS
    nb = SSM_GROUPS // SSM_BLOCK
    y = y.reshape(nc, SSM_ROWS, nb, SSM_CHUNK, SSM_BLOCK * SSM_GROUP)[:, :b]
    return y.transpose(1, 0, 3, 2, 4).reshape(b, nc * SSM_CHUNK, SSM_WIDTH)


def _mix_kernel(x_ref, mod_ref, o_ref, y_ref, sg_ref, wo_ref, wglu_ref, wout_ref, x1_ref):
    a = jnp.dot(o_ref[0], wo_ref[...], preferred_element_type=_F32)
    yg = jax.nn.gelu(y_ref[0].astype(_F32))
    glu = jnp.dot(yg.astype(_BF16), wglu_ref[...], preferred_element_type=_F32)
    s = glu[:, :D_MODEL] * jax.nn.sigmoid(glu[:, D_MODEL:])
    sg = sg_ref[0].astype(_F32)
    merged = sg[:, :D_MODEL] * a + sg[:, D_MODEL:] * s
    out = jnp.dot(merged.astype(_BF16), wout_ref[...], preferred_element_type=_F32)
    g1 = mod_ref[0][2:3, :]
    x1_ref[0] = x_ref[0] + g1 * out


def _mix_call(x, mod6, o, y, sg, wo, wglu, wout, tm):
    b, s, d = x.shape
    tok = lambda w: pl.BlockSpec((1, tm, w), lambda i, j: (i, j, 0))
    return pl.pallas_call(
        _mix_kernel,
        grid=(b, s // tm),
        in_specs=[tok(d), pl.BlockSpec((1, 6, d), lambda i, j: (i, 0, 0)),
                  tok(o.shape[2]), tok(y.shape[2]), tok(sg.shape[2]),
                  _const_spec(wo.shape), _const_spec(wglu.shape), _const_spec(wout.shape)],
        out_specs=tok(d),
        out_shape=jax.ShapeDtypeStruct((b, s, d), _F32),
        compiler_params=_params(("parallel", "parallel")),
        name="mix",
    )(x, mod6, o, y, sg, wo, wglu, wout)


def _ffn_kernel(x_ref, prev_ref, next_ref, mod_ref, g2_ref, wup_ref, cw_ref, cb_ref, wdn_ref,
                out_ref, acc_ref, *, n_chunks):
    j = pl.program_id(1)
    nj = pl.num_programs(1)
    tm = x_ref.shape[1]
    mod = mod_ref[0]
    sh2 = mod[3:4, :]
    sc2 = mod[4:5, :]
    g2 = mod[5:6, :]
    gain = g2_ref[...]

    def prenorm(v):
        return ((_rms(v, D_MODEL) * gain) * (1.0 + sc2) + sh2).astype(_BF16)

    x = x_ref[0]
    h = prenorm(jnp.concatenate([x, prev_ref[0], next_ref[0]], axis=0))
    has_prev = (j > 0).astype(_F32)
    has_next = (j < nj - 1).astype(_F32)
    row = lax.broadcasted_iota(jnp.int32, (8, 2 * FFN_CHUNK), 0)

    def up(ci):
        co = ci * 2 * FFN_CHUNK
        return jnp.dot(h, wup_ref[:, co:co + 2 * FFN_CHUNK], preferred_element_type=_F32)

    def gated(ci, pe):
        co = ci * 2 * FFN_CHUNK
        p = pe[:tm]
        before = pltpu.roll(p, 1, 0)
        after = pltpu.roll(p, tm - 1, 0)
        first = jnp.where(row == 0, pe[tm + 7:tm + 8] * has_prev, before[:8])
        last = jnp.where(row == 7, pe[tm + 8:tm + 9] * has_next, after[tm - 8:])
        before = jnp.concatenate([first, before[8:]], axis=0)
        after = jnp.concatenate([after[:tm - 8], last], axis=0)
        cw = cw_ref[:, co:co + 2 * FFN_CHUNK]
        uc = before * cw[0:1, :] + p * cw[1:2, :] + after * cw[2:3, :] + cb_ref[:, co:co + 2 * FFN_CHUNK]
        val = uc[:, :FFN_CHUNK]
        gate = uc[:, FFN_CHUNK:]
        return (gate * jax.nn.sigmoid(gate) * val).astype(_BF16)

    pe = up(0)
    for ci in range(n_chunks):
        nxt = up(ci + 1) if ci + 1 < n_chunks else None
        act = gated(ci, pe)
        dn = jnp.dot(act, wdn_ref[ci * FFN_CHUNK:(ci + 1) * FFN_CHUNK, :], preferred_element_type=_F32)
        if ci == 0:
            acc_ref[...] = dn
        else:
            acc_ref[...] += dn
        pe = nxt
    out_ref[0] = x + g2 * acc_ref[...]


def _ffn_call(x1, mod6, g2, wup, cw, cb, wdn, tm):
    b, s, d = x1.shape
    n_chunks = wdn.shape[0] // FFN_CHUNK
    hb = tm // 8
    last = s // 8 - 1
    return pl.pallas_call(
        functools.partial(_ffn_kernel, n_chunks=n_chunks),
        grid=(b, s // tm),
        in_specs=[
            pl.BlockSpec((1, tm, d), lambda i, j: (i, j, 0)),
            pl.BlockSpec((1, 8, d), lambda i, j: (i, jnp.maximum(j * hb - 1, 0), 0)),
            pl.BlockSpec((1, 8, d), lambda i, j: (i, jnp.minimum((j + 1) * hb, last), 0)),
            pl.BlockSpec((1, 6, d), lambda i, j: (i, 0, 0)),
            _const_spec(g2.shape), _const_spec(wup.shape), _const_spec(cw.shape),
            _const_spec(cb.shape), _const_spec(wdn.shape),
        ],
        out_specs=pl.BlockSpec((1, tm, d), lambda i, j: (i, j, 0)),
        out_shape=jax.ShapeDtypeStruct((b, s, d), _F32),
        scratch_shapes=[pltpu.VMEM((tm, d), _F32)],
        compiler_params=_params(("parallel", "arbitrary")),
        name="ffn",
    )(x1, x1, x1, mod6, g2, wup, cw, cb, wdn)


def _rope_tables(s):
    rows = s // GRID_W
    row = jnp.repeat(jnp.arange(rows), GRID_W)
    col = jnp.tile(jnp.arange(GRID_W), rows)
    pairs = QK_ROPE // 4
    freqs = ROPE_THETA ** (-jnp.arange(pairs, dtype=_F32) / pairs)
    ang = jnp.concatenate([row[:, None] * freqs, col[:, None] * freqs], axis=-1)
    cos, sin = jnp.cos(ang), jnp.sin(ang)
    z = lambda w: jnp.zeros((s, w), _F32)
    ctab = jnp.concatenate([jnp.ones((s, QK_NOPE), _F32), cos, cos, z(HEAD_PAD - QK_DIM)], axis=-1)
    stab = jnp.concatenate([z(QK_NOPE), -sin, sin, z(HEAD_PAD - QK_DIM)], axis=-1)
    return ctab, stab


def _identity_tables(s):
    ctab = jnp.concatenate([jnp.ones((s, QK_DIM), _F32), jnp.zeros((s, HEAD_PAD - QK_DIM), _F32)], axis=-1)
    return ctab, jnp.zeros((s, HEAD_PAD), _F32)


def _swap_rope_halves(w):
    half = QK_ROPE // 2
    return jnp.concatenate([jnp.zeros_like(w[..., :QK_NOPE]), w[..., QK_NOPE + half:QK_DIM],
                            w[..., QK_NOPE:QK_NOPE + half]], axis=-1)


def _pick_tile(n, pref):
    t = min(n, pref)
    while n % t:
        t //= 2
    return t


def kernel(x, c, ctx, c_ctx, w_mod, b_mod, norm1_g, norm2_g, w_in, q_a_g, w_uq, kv_a_g, w_ukv, q_norm_g, k_norm_g, w_o_attn, lam_re_f, lam_im_f, log_dt_f, c_re_f, c_im_f, lam_re_b, lam_im_b, log_dt_b, c_re_b, c_im_b, b_re, b_im, d_skip, w_glu, w_out, w_up, conv_w, conv_b, w_down):
    b, s, d = x.shape
    n_ctx = ctx.shape[1]
    depth = w_mod.shape[0]
    assert depth == 1, "context update between layers is not implemented"
    l = 0

    wi = w_in[l]
    o1, o2, o3, o4 = Q_LORA, Q_LORA + KV_LORA, Q_LORA + KV_LORA + QK_ROPE, Q_LORA + KV_LORA + QK_ROPE + SSM_WIDTH
    zc = lambda w: jnp.zeros((d, w), wi.dtype)
    half = QK_ROPE // 2
    win = jnp.concatenate([wi[:, :o1], wi[:, o1:o2], wi[:, o3:o4], wi[:, o4:],
                           zc(QK_NOPE), wi[:, o2:o3], zc(HEAD_PAD - QK_DIM),
                           zc(QK_NOPE), wi[:, o2 + half:o3], wi[:, o2:o2 + half], zc(HEAD_PAD - QK_DIM)],
                          axis=-1).astype(_BF16)
    pad_head = lambda w: jnp.pad(w, [(0, 0)] * (w.ndim - 1) + [(0, HEAD_PAD - QK_DIM)])
    wq3 = w_uq[l].reshape(Q_LORA, N_HEADS, QK_DIM)
    wuq = jnp.concatenate([pad_head(wq3).reshape(Q_LORA, N_HEADS * HEAD_PAD),
                           pad_head(_swap_rope_halves(wq3)).reshape(Q_LORA, N_HEADS * HEAD_PAD)],
                          axis=-1).astype(_BF16)
    wkv = w_ukv[l].reshape(KV_LORA, N_HEADS, QK_NOPE + V_DIM)
    wk = jnp.pad(wkv[:, :, :QK_NOPE], ((0, 0), (0, 0), (0, HEAD_PAD - QK_NOPE))).reshape(KV_LORA, N_HEADS * HEAD_PAD)
    wv = wkv[:, :, QK_NOPE:].reshape(KV_LORA, N_HEADS * V_DIM)
    wukv = jnp.concatenate([wk, wv], axis=-1).astype(_BF16)
    padg = lambda g: jnp.stack([pad_head(g), pad_head(_swap_rope_halves(g))], axis=0)
    qng, kng = padg(q_norm_g[l]), padg(k_norm_g[l])
    g1 = norm1_g[l].reshape(1, d)
    g2 = norm2_g[l].reshape(1, d)
    qag = q_a_g[l].reshape(1, Q_LORA)
    kvag = kv_a_g[l].reshape(1, KV_LORA)
    nch = FFN_HIDDEN // FFN_CHUNK
    pair = lambda w: jnp.concatenate(
        [w[..., :FFN_HIDDEN].reshape(w.shape[:-1] + (nch, 1, FFN_CHUNK)),
         w[..., FFN_HIDDEN:].reshape(w.shape[:-1] + (nch, 1, FFN_CHUNK))], axis=-2
    ).reshape(w.shape[:-1] + (2 * FFN_HIDDEN,))
    wup = pair(w_up[l]).astype(_BF16)
    cw = pair(conv_w[l])
    cb = pair(conv_b[l].reshape(1, 2 * FFN_HIDDEN))
    wdn = w_down[l].astype(_BF16)
    wo = w_o_attn[l].astype(_BF16)
    wglu = w_glu[l].astype(_BF16)
    wout = w_out[l].astype(_BF16)

    cc = jnp.concatenate([c, c_ctx[None, :], jnp.zeros((8 - b - 1, d), c.dtype)], axis=0)
    mod = _mod_call(cc, w_mod[l], b_mod[l])
    mod_lat = mod[:b].reshape(b, 6, d)
    mod_ctx = jnp.broadcast_to(mod[b].reshape(1, 6, d), (b, 6, d))

    tm = _pick_tile(s, 512)
    tmc = _pick_tile(n_ctx, 512)
    shared = (g1, win, qag, wuq, kvag, wukv, qng, kng)
    qt, k, vt, u, sg = _inproj_call(x, mod_lat, *shared, *_rope_tables(s), tm)
    _, k_c, vt_c, u_c, _ = _inproj_call(ctx, mod_ctx, *shared, *_identity_tables(n_ctx), tmc)

    wtf, wtb, tt, rf, rb, a = _ssm_tables(lam_re_f[l], lam_im_f[l], log_dt_f[l], c_re_f[l], c_im_f[l],
                                          lam_re_b[l], lam_im_b[l], log_dt_b[l], c_re_b[l], c_im_b[l],
                                          b_re[l], b_im[l], d_skip[l])
    zero_init = jnp.zeros((SSM_GROUPS // SSM_BLOCK, 4, SSM_ROWS, SSM_BLOCK * SSM_STATE), _F32)
    _, _, fin_c = _ssm_scan_call(_to_chunks(u_c), wtf, wtb, a, zero_init)
    ucat = _to_chunks(u)
    xf, xb, _ = _ssm_scan_call(ucat, wtf, wtb, a, fin_c)
    y = _from_chunks(_ssm_out_call(ucat, xf, xb, tt, rf, rb), b)

    tq = _pick_tile(s, 512)
    tk = _pick_tile(s, ATTN_TK)
    o = _attn_call(qt, k, vt, k_c, vt_c, tq, tk)

    x1 = _mix_call(x, mod_lat, o, y, sg, wo, wglu, wout, tm)
    return _ffn_call(x1, mod_lat, g2, wup, cw, cb, wdn, tm)
```

```python
import functools
import math

import jax
import jax.numpy as jnp
from jax import lax
from jax.experimental import pallas as pl
from jax.experimental.pallas import tpu as pltpu

D_MODEL = 1024
GRID_W = 64
N_HEADS = 8
QK_NOPE = 64
QK_ROPE = 32
QK_DIM = QK_NOPE + QK_ROPE
V_DIM = 64
Q_LORA = 384
KV_LORA = 256
ROPE_THETA = 10000.0
SSM_WIDTH = 512
SSM_GROUP = 16
SSM_GROUPS = SSM_WIDTH // SSM_GROUP
SSM_STATE = 64
FFN_HIDDEN = 2816
EPS = 1e-6

LANES = 128
HEAD_PAD = LANES
ONES_ROWS = 16
INPROJ_SUB = 128
ATTN_TK = 256
ATTN_HEADS = 2
SSM_CHUNK = 16
SSM_ROWS = 8
SSM_BLOCK = 8
SSM_SEG_CHUNKS = 64
FFN_CHUNK = 256
VMEM_LIMIT = 56 * 1024 * 1024

_HI = lax.Precision.HIGHEST
_F32 = jnp.float32
_BF16 = jnp.bfloat16


def _params(sem):
    return pltpu.CompilerParams(dimension_semantics=sem, vmem_limit_bytes=VMEM_LIMIT)


def _const_spec(shape):
    nd = len(shape)
    return pl.BlockSpec(shape, lambda *_: (0,) * nd)


def _rms(v, width):
    return v * lax.rsqrt(jnp.sum(v * v, axis=-1, keepdims=True) * (1.0 / width) + EPS)


def _mod_kernel(c_ref, w_ref, b_ref, o_ref):
    c = c_ref[...]
    s = c * jax.nn.sigmoid(c)
    o_ref[...] = jnp.dot(s, w_ref[...], precision=_HI, preferred_element_type=_F32) + b_ref[...]


def _mod_call(cc, w_mod, b_mod):
    rows, d = cc.shape
    n = w_mod.shape[1]
    tn = 1024
    return pl.pallas_call(
        _mod_kernel,
        grid=(n // tn,),
        in_specs=[
            pl.BlockSpec((rows, d), lambda j: (0, 0)),
            pl.BlockSpec((d, tn), lambda j: (0, j)),
            pl.BlockSpec((1, tn), lambda j: (0, j)),
        ],
        out_specs=pl.BlockSpec((rows, tn), lambda j: (0, j)),
        out_shape=jax.ShapeDtypeStruct((rows, n), _F32),
        compiler_params=_params(("arbitrary",)),
        name="mod",
    )(cc, w_mod, b_mod.reshape(1, n))


_C_CQ = 0
_C_CKV = _C_CQ + Q_LORA
_C_U = _C_CKV + KV_LORA
_C_GL = _C_U + SSM_WIDTH
_C_KR = _C_GL + 2 * D_MODEL
_C_KRP = _C_KR + LANES
_IN_W = _C_KRP + LANES
_QW = N_HEADS * HEAD_PAD


def _inproj_kernel(x_ref, mod_ref, g1_ref, win_ref, qag_ref, wuq_ref, kvag_ref, wukv_ref,
                   qng_ref, kng_ref, ctab_ref, stab_ref,
                   qt_ref, k_ref, vt_ref, u_ref, sg_ref, *, sub):
    tm = x_ref.shape[1]
    mod = mod_ref[0]
    sh1 = mod[0:1, :]
    sc1 = 1.0 + mod[1:2, :]
    g1 = g1_ref[...]
    qg = qng_ref[...] * (QK_DIM ** -0.5 * math.log2(math.e))
    kg = kng_ref[...]

    def project(i):
        x = x_ref[0, i * sub:(i + 1) * sub, :]
        h = (_rms(x, D_MODEL) * g1) * sc1 + sh1
        return jnp.dot(h.astype(_BF16), win_ref[...], preferred_element_type=_F32)

    def expand(i, proj):
        rows = slice(i * sub, (i + 1) * sub)
        u_ref[0, rows, :] = proj[:, _C_U:_C_GL].astype(_BF16)
        sg_ref[0, rows, :] = jax.nn.sigmoid(proj[:, _C_GL:_C_KR]).astype(_BF16)
        cq = _rms(proj[:, _C_CQ:_C_CKV], Q_LORA) * qag_ref[...]
        qall = jnp.dot(cq.astype(_BF16), wuq_ref[...], preferred_element_type=_F32)
        ckv = _rms(proj[:, _C_CKV:_C_U], KV_LORA) * kvag_ref[...]
        kvall = jnp.dot(ckv.astype(_BF16), wukv_ref[...], preferred_element_type=_F32)
        return qall, kvall, proj[:, _C_KR:_C_KRP], proj[:, _C_KRP:_IN_W]

    def heads(i, qall, kvall, kr, krp):
        rows = slice(i * sub, (i + 1) * sub)
        ctab = ctab_ref[rows, :]
        stab = stab_ref[rows, :]
        qc = qg[0:1] * ctab
        qs = qg[1:2] * stab
        kc = kg[0:1] * ctab
        k_rot = krp * (kg[1:2] * stab)
        for hd in range(N_HEADS):
            lo = hd * HEAD_PAD
            qh = qall[:, lo:lo + HEAD_PAD]
            nq = lax.rsqrt(jnp.sum(qh * qh, axis=-1, keepdims=True) * (1.0 / QK_DIM) + EPS)
            qr = (qh * qc + qall[:, _QW + lo:_QW + lo + HEAD_PAD] * qs) * nq
            qt_ref[0, hd, :, rows] = qr.T.astype(_BF16)
            kh = kvall[:, lo:lo + HEAD_PAD] + kr
            nk = lax.rsqrt(jnp.sum(kh * kh, axis=-1, keepdims=True) * (1.0 / QK_DIM) + EPS)
            k_ref[0, hd, rows, :] = ((kh * kc + k_rot) * nk).astype(_BF16)
        for hp in range(N_HEADS // 2):
            lo = _QW + hp * LANES
            vt = kvall[:, lo:lo + LANES].T.astype(_BF16)
            vt_ref[0, 2 * hp, :, rows] = vt[:V_DIM]
            vt_ref[0, 2 * hp + 1, :, rows] = vt[V_DIM:]

    n_sub = tm // sub
    proj = project(0)
    for i in range(n_sub):
        nxt = project(i + 1) if i + 1 < n_sub else None
        heads(i, *expand(i, proj))
        proj = nxt


def _inproj_call(x, mod6, g1, win, qag, wuq, kvag, wukv, qng, kng, ctab, stab, tm):
    b, s, d = x.shape
    grid = (b, s // tm)
    sub = min(tm, INPROJ_SUB)
    tok = lambda w: pl.BlockSpec((1, tm, w), lambda i, j: (i, j, 0))
    tab = pl.BlockSpec((tm, LANES), lambda i, j: (j, 0))
    return pl.pallas_call(
        functools.partial(_inproj_kernel, sub=sub),
        grid=grid,
        in_specs=[
            tok(d),
            pl.BlockSpec((1, 6, d), lambda i, j: (i, 0, 0)),
            _const_spec(g1.shape), _const_spec(win.shape), _const_spec(qag.shape),
            _const_spec(wuq.shape), _const_spec(kvag.shape), _const_spec(wukv.shape),
            _const_spec(qng.shape), _const_spec(kng.shape),
            tab, tab,
        ],
        out_specs=[
            pl.BlockSpec((1, N_HEADS, HEAD_PAD, tm), lambda i, j: (i, 0, 0, j)),
            pl.BlockSpec((1, N_HEADS, tm, HEAD_PAD), lambda i, j: (i, 0, j, 0)),
            pl.BlockSpec((1, N_HEADS, V_DIM, tm), lambda i, j: (i, 0, 0, j)),
            tok(SSM_WIDTH), tok(2 * D_MODEL),
        ],
        out_shape=[
            jax.ShapeDtypeStruct((b, N_HEADS, HEAD_PAD, s), _BF16),
            jax.ShapeDtypeStruct((b, N_HEADS, s, HEAD_PAD), _BF16),
            jax.ShapeDtypeStruct((b, N_HEADS, V_DIM, s), _BF16),
            jax.ShapeDtypeStruct((b, s, SSM_WIDTH), _BF16),
            jax.ShapeDtypeStruct((b, s, 2 * D_MODEL), _BF16),
        ],
        compiler_params=_params(("parallel", "parallel")),
        name="inproj",
    )(x, mod6, g1, win, qag, wuq, kvag, wukv, qng, kng, ctab, stab)


def _attn_kernel(qt_ref, k_ref, vt_ref, kc_ref, vtc_ref, o_ref, s_a, s_b, c_a, c_b, p_a, p_b,
                 acc_ref, m_ref, al_ref, *, tiles):
    heads = range(ATTN_HEADS)
    n_kt = len(tiles)
    even = (s_a, c_a, p_a)
    odd = (s_b, c_b, p_b)

    def scores(t, j, dst):
        ctx, off, size = tiles[t]
        keys = (kc_ref if ctx else k_ref)[0, j, off:off + size, :]
        s = jnp.dot(keys, qt_ref[0, j], preferred_element_type=_F32).astype(_BF16)
        dst[0][j, :size] = s
        dst[1][j] = jnp.max(s, axis=0, keepdims=True).astype(_F32)

    def weighted(t, j, p_in):
        ctx, off, size = tiles[t]
        vt = (vtc_ref if ctx else vt_ref)[0, j, :, off:off + size]
        lhs = jnp.concatenate([vt, jnp.ones((ONES_ROWS, size), _BF16)], axis=0)
        pv = jnp.dot(lhs, p_in[j, :size], preferred_element_type=_F32)
        acc_ref[j] = acc_ref[j] * al_ref[j] + pv

    def softmax(t, j, cur):
        size = tiles[t][2]
        m = m_ref[j]
        m_new = jnp.maximum(m, cur[1][j])
        al_ref[j] = jnp.exp2(m - m_new)
        m_ref[j] = m_new
        cur[2][j, :size] = jnp.exp2(cur[0][j, :size] - m_new.astype(_BF16))

    def step(t, cur, nxt):
        if t > 0:
            for j in heads:
                weighted(t - 1, j, nxt[2])
        if t + 1 < n_kt:
            for j in heads:
                scores(t + 1, j, nxt)
        for j in heads:
            softmax(t, j, cur)

    for j in heads:
        m_ref[j] = jnp.full(m_ref.shape[1:], -jnp.inf, _F32)
        al_ref[j] = jnp.ones(al_ref.shape[1:], _F32)
        acc_ref[j] = jnp.zeros(acc_ref.shape[1:], _F32)
        scores(0, j, even)

    for t in range(n_kt):
        if t % 2 == 0:
            step(t, even, odd)
        else:
            step(t, odd, even)
    p_last = p_a if n_kt % 2 else p_b
    outs = []
    for j in heads:
        weighted(n_kt - 1, j, p_last)
        acc = acc_ref[j]
        outs.append(acc[:V_DIM] / acc[V_DIM:V_DIM + 1])
    o_ref[0] = jnp.concatenate(outs, axis=0).T.astype(_BF16)


def _attn_call(qt, k, vt, k_c, vt_c, tq, tk):
    b, h, _, s = qt.shape
    n_ctx = k_c.shape[2]
    tkc = min(tk, n_ctx)
    tiles = tuple((False, o, tk) for o in range(0, s, tk)) + tuple((True, o, tkc) for o in range(0, n_ctx, tkc))
    nh = ATTN_HEADS
    group = lambda rows, cols: pl.BlockSpec((1, nh, rows, cols), lambda i, p, j: (i, p, 0, 0))
    return pl.pallas_call(
        functools.partial(_attn_kernel, tiles=tiles),
        grid=(b, h // nh, s // tq),
        in_specs=[
            pl.BlockSpec((1, nh, HEAD_PAD, tq), lambda i, p, j: (i, p, 0, j)),
            group(s, HEAD_PAD), group(V_DIM, s), group(n_ctx, HEAD_PAD), group(V_DIM, n_ctx),
        ],
        out_specs=pl.BlockSpec((1, tq, nh * V_DIM), lambda i, p, j: (i, j, p)),
        out_shape=jax.ShapeDtypeStruct((b, s, h * V_DIM), _BF16),
        scratch_shapes=[
            pltpu.VMEM((nh, tk, tq), _BF16), pltpu.VMEM((nh, tk, tq), _BF16),
            pltpu.VMEM((nh, 1, tq), _F32), pltpu.VMEM((nh, 1, tq), _F32),
            pltpu.VMEM((nh, tk, tq), _BF16), pltpu.VMEM((nh, tk, tq), _BF16),
            pltpu.VMEM((nh, V_DIM + ONES_ROWS, tq), _F32),
            pltpu.VMEM((nh, 1, tq), _F32), pltpu.VMEM((nh, 1, tq), _F32),
        ],
        compiler_params=_params(("parallel", "parallel", "arbitrary")),
        name="attn",
    )(qt, k, vt, k_c, vt_c)


_SB = SSM_BLOCK * SSM_GROUP
_SN = SSM_BLOCK * SSM_STATE


def _cmul(ar, ai, br, bi):
    return ar * br - ai * bi, ar * bi + ai * br


def _ssm_powers(lre, lim, ldt):
    dt = jnp.exp(ldt)
    mag = jnp.exp(lre * dt)
    ar = mag * jnp.cos(lim * dt)
    ai = mag * jnp.sin(lim * dt)
    nr = ar - 1.0
    den = lre * lre + lim * lim
    cfr = (nr * lre + ai * lim) / den
    cfi = (ai * lre - nr * lim) / den
    pr = [jnp.ones_like(ar)]
    pi = [jnp.zeros_like(ar)]
    for _ in range(SSM_CHUNK):
        r, i = _cmul(pr[-1], pi[-1], ar, ai)
        pr.append(r)
        pi.append(i)
    return pr, pi, cfr, cfi


def _same_group():
    row = lax.broadcasted_iota(jnp.int32, (_SB, _SN), 0) // SSM_GROUP
    col = lax.broadcasted_iota(jnp.int32, (_SB, _SN), 1) // SSM_STATE
    return row == col


def _ssmtab_kernel(lre_ref, lim_ref, ldt_ref, mre_ref, mim_ref, dsk_ref, *outs, what):
    same = _same_group()
    zero = jnp.zeros((_SB, _SN), _F32)
    pw = [_ssm_powers(lre_ref[d, 0], lim_ref[d, 0], ldt_ref[d, 0]) for d in range(2)]

    def input_matrix(d, bre, bim):
        _, _, cfr, cfi = pw[d]
        r, i = _cmul(bre, bim, cfr, cfi)
        return jnp.where(same, r, zero), jnp.where(same, i, zero)

    if what == "w":
        (wt_ref,) = outs
        for d in range(2):
            pr, pi = pw[d][:2]
            bbr, bbi = input_matrix(d, mre_ref[0], mim_ref[0])
            for s in range(SSM_CHUNK):
                k = s if d else SSM_CHUNK - 1 - s
                r, i = _cmul(bbr, bbi, pr[k], pi[k])
                wt_ref[d, 0, s * _SB:(s + 1) * _SB, :_SN] = r.astype(_BF16)
                wt_ref[d, 0, s * _SB:(s + 1) * _SB, _SN:] = i.astype(_BF16)
    elif what == "r":
        r_ref, a_ref = outs
        for d in range(2):
            pr, pi = pw[d][:2]
            cr = jnp.where(same, mre_ref[d, 0], zero)
            ci = jnp.where(same, mim_ref[d, 0], zero)
            for t in range(SSM_CHUNK):
                k = SSM_CHUNK - t if d else t + 1
                r, i = _cmul(cr, ci, pr[k], pi[k])
                r_ref[d, 0, :_SN, t * _SB:(t + 1) * _SB] = r.T.astype(_BF16)
                r_ref[d, 0, _SN:, t * _SB:(t + 1) * _SB] = (-i).T.astype(_BF16)
            a_ref[0, 2 * d:2 * d + 1, :] = pr[SSM_CHUNK]
            a_ref[0, 2 * d + 1:2 * d + 2, :] = pi[SSM_CHUNK]
    else:
        cre_ref, cim_ref, tt_ref, lag_ref = outs
        dn = (((1,), (1,)), ((), ()))
        for d in range(2):
            pr, pi = pw[d][:2]
            bbr, bbi = input_matrix(d, mre_ref[0], mim_ref[0])
            cr = jnp.where(same, cre_ref[d, 0], zero)
            ci = jnp.where(same, cim_ref[d, 0], zero)
            for k in range(SSM_CHUNK):
                r, i = _cmul(bbr, bbi, pr[k], pi[k])
                kk = (lax.dot_general(r, cr, dn, precision=_HI, preferred_element_type=_F32)
                      - lax.dot_general(i, ci, dn, precision=_HI, preferred_element_type=_F32))
                lag_ref[d * SSM_CHUNK + k] = kk
        row = lax.broadcasted_iota(jnp.int32, (_SB, _SB), 0)
        col = lax.broadcasted_iota(jnp.int32, (_SB, _SB), 1)
        lag_ref[0] = lag_ref[0] + lag_ref[SSM_CHUNK] + jnp.where(row == col, dsk_ref[0], 0.0)
        for s in range(SSM_CHUNK):
            for t in range(SSM_CHUNK):
                k = t - s if t >= s else SSM_CHUNK + s - t
                tt_ref[0, s * _SB:(s + 1) * _SB, t * _SB:(t + 1) * _SB] = lag_ref[k].astype(_BF16)


def _ssmtab_call(what, lre, lim, ldt, mre, mim, dsk, cre=None, cim=None):
    nb = mre.shape[-3]
    row = pl.BlockSpec((2, 1, 1, _SN), lambda j: (0, j, 0, 0))
    mat1 = pl.BlockSpec((1, _SB, _SN), lambda j: (j, 0, 0))
    mat2 = pl.BlockSpec((2, 1, _SB, _SN), lambda j: (0, j, 0, 0))
    dspec = pl.BlockSpec((1, 1, _SB), lambda j: (j, 0, 0))
    l = SSM_CHUNK
    scratch = []
    if what == "w":
        in_specs, args = [row, row, row, mat1, mat1, dspec], (lre, lim, ldt, mre, mim, dsk)
        out_specs = [pl.BlockSpec((2, 1, l * _SB, 2 * _SN), lambda j: (0, j, 0, 0))]
        out_shape = [jax.ShapeDtypeStruct((2, nb, l * _SB, 2 * _SN), _BF16)]
    elif what == "r":
        in_specs, args = [row, row, row, mat2, mat2, dspec], (lre, lim, ldt, mre, mim, dsk)
        out_specs = [pl.BlockSpec((2, 1, 2 * _SN, l * _SB), lambda j: (0, j, 0, 0)),
                     pl.BlockSpec((1, 4, _SN), lambda j: (j, 0, 0))]
        out_shape = [jax.ShapeDtypeStruct((2, nb, 2 * _SN, l * _SB), _BF16),
                     jax.ShapeDtypeStruct((nb, 4, _SN), _F32)]
    else:
        in_specs, args = [row, row, row, mat1, mat1, dspec, mat2, mat2], (lre, lim, ldt, mre, mim, dsk, cre, cim)
        out_specs = [pl.BlockSpec((1, l * _SB, l * _SB), lambda j: (j, 0, 0))]
        out_shape = [jax.ShapeDtypeStruct((nb, l * _SB, l * _SB), _BF16)]
        scratch = [pltpu.VMEM((2 * l, _SB, _SB), _F32)]
    return pl.pallas_call(
        functools.partial(_ssmtab_kernel, what=what),
        grid=(nb,),
        in_specs=in_specs,
        out_specs=out_specs,
        out_shape=out_shape,
        scratch_shapes=scratch,
        compiler_params=_params(("parallel",)),
        name="ssmtab_" + what,
    )(*args)


def _ssm_tables(lam_re_f, lam_im_f, log_dt_f, c_re_f, c_im_f,
                lam_re_b, lam_im_b, log_dt_b, c_re_b, c_im_b, b_re, b_im, d_skip):
    g, n, c, gb = SSM_GROUPS, SSM_STATE, SSM_GROUP, SSM_BLOCK
    nb = g // gb
    rows = lambda f, b: jnp.stack([f, b]).reshape(2, nb, 1, gb * n)
    lre = rows(lam_re_f, lam_re_b)
    lim = rows(lam_im_f, lam_im_b)
    ldt = rows(jnp.broadcast_to(log_dt_f[:, None], (g, n)), jnp.broadcast_to(log_dt_b[:, None], (g, n)))
    tiled = lambda m: jnp.tile(m.reshape(nb, gb * c, n), (1, 1, gb))
    btre = tiled(b_re.transpose(0, 2, 1))
    btim = tiled(b_im.transpose(0, 2, 1))
    cre = jnp.stack([tiled(c_re_f), tiled(c_re_b)])
    cim = jnp.stack([tiled(c_im_f), tiled(c_im_b)])
    dsk = d_skip.reshape(nb, 1, gb * c)
    (wt,) = _ssmtab_call("w", lre, lim, ldt, btre, btim, dsk)
    r, a = _ssmtab_call("r", lre, lim, ldt, cre, cim, dsk)
    (tt,) = _ssmtab_call("t", lre, lim, ldt, btre, btim, dsk, cre, cim)
    return wt[0], wt[1], tt, r[0], r[1], a


def _ssm_scan_kernel(uf_ref, ub_ref, wtf_ref, wtb_ref, a_ref, init_ref, xf_ref, xb_ref, fin_ref,
                     ef, eb, st, *, segc):
    i = pl.program_id(1)
    half = SSM_BLOCK * SSM_STATE

    @pl.when(i == 0)
    def _():
        st[...] = init_ref[0]

    ef[...] = jnp.dot(uf_ref[...], wtf_ref[0], preferred_element_type=_F32)
    eb[...] = jnp.dot(ub_ref[...], wtb_ref[0], preferred_element_type=_F32)
    a = a_ref[0]
    afr, afi, abr, abi = (jnp.broadcast_to(a[k:k + 1, :], (SSM_ROWS, half)) for k in range(4))

    def step(c, carry):
        fr, fi, br, bi = carry
        rf = pl.ds(pl.multiple_of(c * SSM_ROWS, SSM_ROWS), SSM_ROWS)
        rb = pl.ds(pl.multiple_of((segc - 1 - c) * SSM_ROWS, SSM_ROWS), SSM_ROWS)
        xf_ref[0, rf, :] = jnp.concatenate([fr, fi], axis=-1)
        xb_ref[0, rb, :] = jnp.concatenate([br, bi], axis=-1)
        e_f = ef[rf, :]
        e_b = eb[rb, :]
        nfr = afr * fr - afi * fi + e_f[:, :half]
        nfi = afr * fi + afi * fr + e_f[:, half:]
        nbr = abr * br - abi * bi + e_b[:, :half]
        nbi = abr * bi + abi * br + e_b[:, half:]
        return nfr, nfi, nbr, nbi

    fin = lax.fori_loop(0, segc, step, tuple(st[k] for k in range(4)))
    for k in range(4):
        st[k] = fin[k]
        fin_ref[0, k] = fin[k]


def _ssm_scan_call(ucat, wtf, wtb, a, init):
    rows, width = ucat.shape
    nb = wtf.shape[0]
    kw = width // nb
    half = SSM_BLOCK * SSM_STATE
    nc = rows // SSM_ROWS
    segc = min(nc, SSM_SEG_CHUNKS)
    nseg = nc // segc
    seg_rows = segc * SSM_ROWS
    f32 = lambda *shape: jax.ShapeDtypeStruct(shape, _F32)
    return pl.pallas_call(
        functools.partial(_ssm_scan_kernel, segc=segc),
        grid=(nb, nseg),
        in_specs=[
            pl.BlockSpec((seg_rows, kw), lambda j, i: (i, j)),
            pl.BlockSpec((seg_rows, kw), lambda j, i: (nseg - 1 - i, j)),
            pl.BlockSpec((1, kw, 2 * half), lambda j, i: (j, 0, 0)),
            pl.BlockSpec((1, kw, 2 * half), lambda j, i: (j, 0, 0)),
            pl.BlockSpec((1, 4, half), lambda j, i: (j, 0, 0)),
            pl.BlockSpec((1, 4, SSM_ROWS, half), lambda j, i: (j, 0, 0, 0)),
        ],
        out_specs=[
            pl.BlockSpec((1, seg_rows, 2 * half), lambda j, i: (j, i, 0)),
            pl.BlockSpec((1, seg_rows, 2 * half), lambda j, i: (j, nseg - 1 - i, 0)),
            pl.BlockSpec((1, 4, SSM_ROWS, half), lambda j, i: (j, 0, 0, 0)),
        ],
        out_shape=[f32(nb, rows, 2 * half), f32(nb, rows, 2 * half), f32(nb, 4, SSM_ROWS, half)],
        scratch_shapes=[pltpu.VMEM((seg_rows, 2 * half), _F32), pltpu.VMEM((seg_rows, 2 * half), _F32),
                        pltpu.VMEM((4, SSM_ROWS, half), _F32)],
        compiler_params=_params(("parallel", "arbitrary")),
        name="ssm_scan",
    )(ucat, ucat, wtf, wtb, a, init)


def _ssm_out_kernel(u_ref, xf_ref, xb_ref, tt_ref, rf_ref, rb_ref, y_ref):
    y = jnp.dot(u_ref[...], tt_ref[0], preferred_element_type=_F32)
    y = y + jnp.dot(xf_ref[0].astype(_BF16), rf_ref[0], preferred_element_type=_F32)
    y = y + jnp.dot(xb_ref[0].astype(_BF16), rb_ref[0], preferred_element_type=_F32)
    y_ref[...] = y.astype(_BF16)


def _ssm_out_call(ucat, xf, xb, tt, rf, rb):
    rows, width = ucat.shape
    nb = tt.shape[0]
    kw = width // nb
    sw = xf.shape[2]
    tr = _pick_tile(rows, 512)
    return pl.pallas_call(
        _ssm_out_kernel,
        grid=(nb, rows // tr),
        in_specs=[
            pl.BlockSpec((tr, kw), lambda j, i: (i, j)),
            pl.BlockSpec((1, tr, sw), lambda j, i: (j, i, 0)),
            pl.BlockSpec((1, tr, sw), lambda j, i: (j, i, 0)),
            pl.BlockSpec((1, kw, kw), lambda j, i: (j, 0, 0)),
            pl.BlockSpec((1, sw, kw), lambda j, i: (j, 0, 0)),
            pl.BlockSpec((1, sw, kw), lambda j, i: (j, 0, 0)),
        ],
        out_specs=pl.BlockSpec((tr, kw), lambda j, i: (i, j)),
        out_shape=jax.ShapeDtypeStruct((rows, width), _BF16),
        compiler_params=_params(("parallel", "parallel")),
        name="ssm_out",
    )(ucat, xf, xb, tt, rf, rb)


def _to_chunks(u):
    b, s, _ = u.shape
    nc = s // SSM_CHUNK
    nb = SSM_GROUPS // SSM_BLOCK
    uc = u.reshape(b, nc, SSM_CHUNK, nb, SSM_BLOCK * SSM_GROUP).transpose(1, 0, 3, 2, 4)
    uc = jnp.pad(uc, ((0, 0), (0, SSM_ROWS - b), (0, 0), (0, 0), (0, 0)))
    return uc.reshape(nc * SSM_ROWS, SSM_CHUNK * SSM_WIDTH)


def _from_chunks(y, b):
    rows, _ = y.shape
    nc = rows // SSM_ROWS
    nb = SSM_GROUPS // SSM_BLOCK
    y = y.reshape(nc, SSM_ROWS, nb, SSM_CHUNK, SSM_BLOCK * SSM_GROUP)[:, :b]
    return y.transpose(1, 0, 3, 2, 4).reshape(b, nc * SSM_CHUNK, SSM_WIDTH)


def _mix_kernel(x_ref, mod_ref, o_ref, y_ref, sg_ref, wo_ref, wglu_ref, wout_ref, x1_ref):
    a = jnp.dot(o_ref[0], wo_ref[...], preferred_element_type=_F32)
    yg = jax.nn.gelu(y_ref[0].astype(_F32))
    glu = jnp.dot(yg.astype(_BF16), wglu_ref[...], preferred_element_type=_F32)
    s = glu[:, :D_MODEL] * jax.nn.sigmoid(glu[:, D_MODEL:])
    sg = sg_ref[0].astype(_F32)
    merged = sg[:, :D_MODEL] * a + sg[:, D_MODEL:] * s
    out = jnp.dot(merged.astype(_BF16), wout_ref[...], preferred_element_type=_F32)
    g1 = mod_ref[0][2:3, :]
    x1_ref[0] = x_ref[0] + g1 * out


def _mix_call(x, mod6, o, y, sg, wo, wglu, wout, tm):
    b, s, d = x.shape
    tok = lambda w: pl.BlockSpec((1, tm, w), lambda i, j: (i, j, 0))
    return pl.pallas_call(
        _mix_kernel,
        grid=(b, s // tm),
        in_specs=[tok(d), pl.BlockSpec((1, 6, d), lambda i, j: (i, 0, 0)),
                  tok(o.shape[2]), tok(y.shape[2]), tok(sg.shape[2]),
                  _const_spec(wo.shape), _const_spec(wglu.shape), _const_spec(wout.shape)],
        out_specs=tok(d),
        out_shape=jax.ShapeDtypeStruct((b, s, d), _F32),
        compiler_params=_params(("parallel", "parallel")),
        name="mix",
    )(x, mod6, o, y, sg, wo, wglu, wout)


def _ffn_kernel(x_ref, prev_ref, next_ref, mod_ref, g2_ref, wup_ref, cw_ref, cb_ref, wdn_ref,
                out_ref, acc_ref, *, n_chunks):
    j = pl.program_id(1)
    nj = pl.num_programs(1)
    tm = x_ref.shape[1]
    mod = mod_ref[0]
    sh2 = mod[3:4, :]
    sc2 = mod[4:5, :]
    g2 = mod[5:6, :]
    gain = g2_ref[...]

    def prenorm(v):
        return ((_rms(v, D_MODEL) * gain) * (1.0 + sc2) + sh2).astype(_BF16)

    x = x_ref[0]
    h = prenorm(jnp.concatenate([x, prev_ref[0], next_ref[0]], axis=0))
    has_prev = (j > 0).astype(_F32)
    has_next = (j < nj - 1).astype(_F32)
    row = lax.broadcasted_iota(jnp.int32, (8, FFN_CHUNK), 0)

    def up(ci):
        lo = ci * FFN_CHUNK
        return tuple(jnp.dot(h, wup_ref[:, o + lo:o + lo + FFN_CHUNK], preferred_element_type=_F32)
                     for o in (0, FFN_HIDDEN))

    def conv(pe, lo):
        p = pe[:tm]
        before = pltpu.roll(p, 1, 0)
        after = pltpu.roll(p, tm - 1, 0)
        first = jnp.where(row == 0, pe[tm + 7:tm + 8] * has_prev, before[:8])
        last = jnp.where(row == 7, pe[tm + 8:tm + 9] * has_next, after[tm - 8:])
        before = jnp.concatenate([first, before[8:]], axis=0)
        after = jnp.concatenate([after[:tm - 8], last], axis=0)
        cw = cw_ref[:, lo:lo + FFN_CHUNK]
        return before * cw[0:1, :] + p * cw[1:2, :] + after * cw[2:3, :] + cb_ref[:, lo:lo + FFN_CHUNK]

    def gated(ci, pe):
        val = conv(pe[0], ci * FFN_CHUNK)
        gate = conv(pe[1], FFN_HIDDEN + ci * FFN_CHUNK)
        return (gate * jax.nn.sigmoid(gate) * val).astype(_BF16)

    pe = up(0)
    for ci in range(n_chunks):
        nxt = up(ci + 1) if ci + 1 < n_chunks else None
        act = gated(ci, pe)
        dn = jnp.dot(act, wdn_ref[ci * FFN_CHUNK:(ci + 1) * FFN_CHUNK, :], preferred_element_type=_F32)
        if ci == 0:
            acc_ref[...] = dn
        else:
            acc_ref[...] += dn
        pe = nxt
    out_ref[0] = x + g2 * acc_ref[...]


def _ffn_call(x1, mod6, g2, wup, cw, cb, wdn, tm):
    b, s, d = x1.shape
    n_chunks = wdn.shape[0] // FFN_CHUNK
    hb = tm // 8
    last = s // 8 - 1
    return pl.pallas_call(
        functools.partial(_ffn_kernel, n_chunks=n_chunks),
        grid=(b, s // tm),
        in_specs=[
            pl.BlockSpec((1, tm, d), lambda i, j: (i, j, 0)),
            pl.BlockSpec((1, 8, d), lambda i, j: (i, jnp.maximum(j * hb - 1, 0), 0)),
            pl.BlockSpec((1, 8, d), lambda i, j: (i, jnp.minimum((j + 1) * hb, last), 0)),
            pl.BlockSpec((1, 6, d), lambda i, j: (i, 0, 0)),
            _const_spec(g2.shape), _const_spec(wup.shape), _const_spec(cw.shape),
            _const_spec(cb.shape), _const_spec(wdn.shape),
        ],
        out_specs=pl.BlockSpec((1, tm, d), lambda i, j: (i, j, 0)),
        out_shape=jax.ShapeDtypeStruct((b, s, d), _F32),
        scratch_shapes=[pltpu.VMEM((tm, d), _F32)],
        compiler_params=_params(("parallel", "arbitrary")),
        name="ffn",
    )(x1, x1, x1, mod6, g2, wup, cw, cb, wdn)


def _rope_tables(s):
    rows = s // GRID_W
    row = jnp.repeat(jnp.arange(rows), GRID_W)
    col = jnp.tile(jnp.arange(GRID_W), rows)
    pairs = QK_ROPE // 4
    freqs = ROPE_THETA ** (-jnp.arange(pairs, dtype=_F32) / pairs)
    ang = jnp.concatenate([row[:, None] * freqs, col[:, None] * freqs], axis=-1)
    cos, sin = jnp.cos(ang), jnp.sin(ang)
    z = lambda w: jnp.zeros((s, w), _F32)
    ctab = jnp.concatenate([jnp.ones((s, QK_NOPE), _F32), cos, cos, z(HEAD_PAD - QK_DIM)], axis=-1)
    stab = jnp.concatenate([z(QK_NOPE), -sin, sin, z(HEAD_PAD - QK_DIM)], axis=-1)
    return ctab, stab


def _identity_tables(s):
    ctab = jnp.concatenate([jnp.ones((s, QK_DIM), _F32), jnp.zeros((s, HEAD_PAD - QK_DIM), _F32)], axis=-1)
    return ctab, jnp.zeros((s, HEAD_PAD), _F32)


def _swap_rope_halves(w):
    half = QK_ROPE // 2
    return jnp.concatenate([jnp.zeros_like(w[..., :QK_NOPE]), w[..., QK_NOPE + half:QK_DIM],
                            w[..., QK_NOPE:QK_NOPE + half]], axis=-1)


def _pick_tile(n, pref):
    t = min(n, pref)
    while n % t:
        t //= 2
    return t


def kernel(x, c, ctx, c_ctx, w_mod, b_mod, norm1_g, norm2_g, w_in, q_a_g, w_uq, kv_a_g, w_ukv, q_norm_g, k_norm_g, w_o_attn, lam_re_f, lam_im_f, log_dt_f, c_re_f, c_im_f, lam_re_b, lam_im_b, log_dt_b, c_re_b, c_im_b, b_re, b_im, d_skip, w_glu, w_out, w_up, conv_w, conv_b, w_down):
    b, s, d = x.shape
    n_ctx = ctx.shape[1]
    depth = w_mod.shape[0]
    assert depth == 1, "context update between layers is not implemented"
    l = 0

    wi = w_in[l]
    o1, o2, o3, o4 = Q_LORA, Q_LORA + KV_LORA, Q_LORA + KV_LORA + QK_ROPE, Q_LORA + KV_LORA + QK_ROPE + SSM_WIDTH
    zc = lambda w: jnp.zeros((d, w), wi.dtype)
    half = QK_ROPE // 2
    win = jnp.concatenate([wi[:, :o1], wi[:, o1:o2], wi[:, o3:o4], wi[:, o4:],
                           zc(QK_NOPE), wi[:, o2:o3], zc(HEAD_PAD - QK_DIM),
                           zc(QK_NOPE), wi[:, o2 + half:o3], wi[:, o2:o2 + half], zc(HEAD_PAD - QK_DIM)],
                          axis=-1).astype(_BF16)
    pad_head = lambda w: jnp.pad(w, [(0, 0)] * (w.ndim - 1) + [(0, HEAD_PAD - QK_DIM)])
    wq3 = w_uq[l].reshape(Q_LORA, N_HEADS, QK_DIM)
    wuq = jnp.concatenate([pad_head(wq3).reshape(Q_LORA, N_HEADS * HEAD_PAD),
                           pad_head(_swap_rope_halves(wq3)).reshape(Q_LORA, N_HEADS * HEAD_PAD)],
                          axis=-1).astype(_BF16)
    wkv = w_ukv[l].reshape(KV_LORA, N_HEADS, QK_NOPE + V_DIM)
    wk = jnp.pad(wkv[:, :, :QK_NOPE], ((0, 0), (0, 0), (0, HEAD_PAD - QK_NOPE))).reshape(KV_LORA, N_HEADS * HEAD_PAD)
    wv = wkv[:, :, QK_NOPE:].reshape(KV_LORA, N_HEADS * V_DIM)
    wukv = jnp.concatenate([wk, wv], axis=-1).astype(_BF16)
    padg = lambda g: jnp.stack([pad_head(g), pad_head(_swap_rope_halves(g))], axis=0)
    qng, kng = padg(q_norm_g[l]), padg(k_norm_g[l])
    g1 = norm1_g[l].reshape(1, d)
    g2 = norm2_g[l].reshape(1, d)
    qag = q_a_g[l].reshape(1, Q_LORA)
    kvag = kv_a_g[l].reshape(1, KV_LORA)
    wup = w_up[l].astype(_BF16)
    cw = conv_w[l]
    cb = conv_b[l].reshape(1, 2 * FFN_HIDDEN)
    wdn = w_down[l].astype(_BF16)
    wo = w_o_attn[l].astype(_BF16)
    wglu = w_glu[l].astype(_BF16)
    wout = w_out[l].astype(_BF16)

    cc = jnp.concatenate([c, c_ctx[None, :], jnp.zeros((8 - b - 1, d), c.dtype)], axis=0)
    mod = _mod_call(cc, w_mod[l], b_mod[l])
    mod_lat = mod[:b].reshape(b, 6, d)
    mod_ctx = jnp.broadcast_to(mod[b].reshape(1, 6, d), (b, 6, d))

    tm = _pick_tile(s, 512)
    tmc = _pick_tile(n_ctx, 512)
    shared = (g1, win, qag, wuq, kvag, wukv, qng, kng)
    qt, k, vt, u, sg = _inproj_call(x, mod_lat, *shared, *_rope_tables(s), tm)
    _, k_c, vt_c, u_c, _ = _inproj_call(ctx, mod_ctx, *shared, *_identity_tables(n_ctx), tmc)

    wtf, wtb, tt, rf, rb, a = _ssm_tables(lam_re_f[l], lam_im_f[l], log_dt_f[l], c_re_f[l], c_im_f[l],
                                          lam_re_b[l], lam_im_b[l], log_dt_b[l], c_re_b[l], c_im_b[l],
                                          b_re[l], b_im[l], d_skip[l])
    zero_init = jnp.zeros((SSM_GROUPS // SSM_BLOCK, 4, SSM_ROWS, SSM_BLOCK * SSM_STATE), _F32)
    _, _, fin_c = _ssm_scan_call(_to_chunks(u_c), wtf, wtb, a, zero_init)
    ucat = _to_chunks(u)
    xf, xb, _ = _ssm_scan_call(ucat, wtf, wtb, a, fin_c)
    y = _from_chunks(_ssm_out_call(ucat, xf, xb, tt, rf, rb), b)

    tq = _pick_tile(s, 512)
    tk = _pick_tile(s, ATTN_TK)
    o = _attn_call(qt, k, vt, k_c, vt_c, tq, tk)

    x1 = _mix_call(x, mod_lat, o, y, sg, wo, wglu, wout, tm)
    return _ffn_call(x1, mod_lat, g2, wup, cw, cb, wdn, tm)
```

```python
import functools
import math

import jax
import jax.numpy as jnp
from jax import lax
from jax.experimental import pallas as pl
from jax.experimental.pallas import tpu as pltpu

D_MODEL = 1024
GRID_W = 64
N_HEADS = 8
QK_NOPE = 64
QK_ROPE = 32
QK_DIM = QK_NOPE + QK_ROPE
V_DIM = 64
Q_LORA = 384
KV_LORA = 256
ROPE_THETA = 10000.0
SSM_WIDTH = 512
SSM_GROUP = 16
SSM_GROUPS = SSM_WIDTH // SSM_GROUP
SSM_STATE = 64
FFN_HIDDEN = 2816
EPS = 1e-6

LANES = 128
HEAD_PAD = LANES
ONES_ROWS = 16
INPROJ_SUB = 128
ATTN_TK = 256
ATTN_HEADS = 2
SSM_CHUNK = 16
SSM_ROWS = 8
SSM_BLOCK = 8
SSM_SEG_CHUNKS = 64
FFN_CHUNK = 256
VMEM_LIMIT = 56 * 1024 * 1024

_HI = lax.Precision.HIGHEST
_F32 = jnp.float32
_BF16 = jnp.bfloat16


def _params(sem):
    return pltpu.CompilerParams(dimension_semantics=sem, vmem_limit_bytes=VMEM_LIMIT)


def _const_spec(shape):
    nd = len(shape)
    return pl.BlockSpec(shape, lambda *_: (0,) * nd)


def _rms(v, width):
    return v * lax.rsqrt(jnp.sum(v * v, axis=-1, keepdims=True) * (1.0 / width) + EPS)


def _mod_kernel(c_ref, w_ref, b_ref, o_ref):
    c = c_ref[...]
    s = c * jax.nn.sigmoid(c)
    o_ref[...] = jnp.dot(s, w_ref[...], precision=_HI, preferred_element_type=_F32) + b_ref[...]


def _mod_call(cc, w_mod, b_mod):
    rows, d = cc.shape
    n = w_mod.shape[1]
    tn = 1024
    return pl.pallas_call(
        _mod_kernel,
        grid=(n // tn,),
        in_specs=[
            pl.BlockSpec((rows, d), lambda j: (0, 0)),
            pl.BlockSpec((d, tn), lambda j: (0, j)),
            pl.BlockSpec((1, tn), lambda j: (0, j)),
        ],
        out_specs=pl.BlockSpec((rows, tn), lambda j: (0, j)),
        out_shape=jax.ShapeDtypeStruct((rows, n), _F32),
        compiler_params=_params(("arbitrary",)),
        name="mod",
    )(cc, w_mod, b_mod.reshape(1, n))


_C_CQ = 0
_C_CKV = _C_CQ + Q_LORA
_C_U = _C_CKV + KV_LORA
_C_GL = _C_U + SSM_WIDTH
_C_KR = _C_GL + 2 * D_MODEL
_C_KRP = _C_KR + LANES
_IN_W = _C_KRP + LANES
_QW = N_HEADS * HEAD_PAD


def _inproj_kernel(x_ref, mod_ref, g1_ref, win_ref, qag_ref, wuq_ref, kvag_ref, wukv_ref,
                   qng_ref, kng_ref, ctab_ref, stab_ref,
                   qt_ref, k_ref, vt_ref, u_ref, sg_ref, *, sub):
    tm = x_ref.shape[1]
    mod = mod_ref[0]
    sh1 = mod[0:1, :]
    sc1 = 1.0 + mod[1:2, :]
    g1 = g1_ref[...]
    qg = qng_ref[...] * (QK_DIM ** -0.5 * math.log2(math.e))
    kg = kng_ref[...]

    def project(i):
        x = x_ref[0, i * sub:(i + 1) * sub, :]
        h = (_rms(x, D_MODEL) * g1) * sc1 + sh1
        return jnp.dot(h.astype(_BF16), win_ref[...], preferred_element_type=_F32)

    def expand(i, proj):
        rows = slice(i * sub, (i + 1) * sub)
        u_ref[0, rows, :] = proj[:, _C_U:_C_GL].astype(_BF16)
        sg_ref[0, rows, :] = jax.nn.sigmoid(proj[:, _C_GL:_C_KR]).astype(_BF16)
        cq = _rms(proj[:, _C_CQ:_C_CKV], Q_LORA) * qag_ref[...]
        qall = jnp.dot(cq.astype(_BF16), wuq_ref[...], preferred_element_type=_F32)
        ckv = _rms(proj[:, _C_CKV:_C_U], KV_LORA) * kvag_ref[...]
        kvall = jnp.dot(ckv.astype(_BF16), wukv_ref[...], preferred_element_type=_F32)
        return qall, kvall, proj[:, _C_KR:_C_KRP], proj[:, _C_KRP:_IN_W]

    def heads(i, qall, kvall, kr, krp):
        rows = slice(i * sub, (i + 1) * sub)
        ctab = ctab_ref[rows, :]
        stab = stab_ref[rows, :]
        qc = qg[0:1] * ctab
        qs = qg[1:2] * stab
        kc = kg[0:1] * ctab
        k_rot = krp * (kg[1:2] * stab)
        for hd in range(N_HEADS):
            lo = hd * HEAD_PAD
            qh = qall[:, lo:lo + HEAD_PAD]
            nq = lax.rsqrt(jnp.sum(qh * qh, axis=-1, keepdims=True) * (1.0 / QK_DIM) + EPS)
            qr = (qh * qc + qall[:, _QW + lo:_QW + lo + HEAD_PAD] * qs) * nq
            qt_ref[0, hd, :, rows] = qr.T.astype(_BF16)
            kh = kvall[:, lo:lo + HEAD_PAD] + kr
            nk = lax.rsqrt(jnp.sum(kh * kh, axis=-1, keepdims=True) * (1.0 / QK_DIM) + EPS)
            k_ref[0, hd, rows, :] = ((kh * kc + k_rot) * nk).astype(_BF16)
        for hp in range(N_HEADS // 2):
            lo = _QW + hp * LANES
            vt = kvall[:, lo:lo + LANES].T.astype(_BF16)
            vt_ref[0, 2 * hp, :, rows] = vt[:V_DIM]
            vt_ref[0, 2 * hp + 1, :, rows] = vt[V_DIM:]

    n_sub = tm // sub
    proj = project(0)
    for i in range(n_sub):
        nxt = project(i + 1) if i + 1 < n_sub else None
        heads(i, *expand(i, proj))
        proj = nxt


def _inproj_call(x, mod6, g1, win, qag, wuq, kvag, wukv, qng, kng, ctab, stab, tm):
    b, s, d = x.shape
    grid = (b, s // tm)
    sub = min(tm, INPROJ_SUB)
    tok = lambda w: pl.BlockSpec((1, tm, w), lambda i, j: (i, j, 0))
    tab = pl.BlockSpec((tm, LANES), lambda i, j: (j, 0))
    return pl.pallas_call(
        functools.partial(_inproj_kernel, sub=sub),
        grid=grid,
        in_specs=[
            tok(d),
            pl.BlockSpec((1, 6, d), lambda i, j: (i, 0, 0)),
            _const_spec(g1.shape), _const_spec(win.shape), _const_spec(qag.shape),
            _const_spec(wuq.shape), _const_spec(kvag.shape), _const_spec(wukv.shape),
            _const_spec(qng.shape), _const_spec(kng.shape),
            tab, tab,
        ],
        out_specs=[
            pl.BlockSpec((1, N_HEADS, HEAD_PAD, tm), lambda i, j: (i, 0, 0, j)),
            pl.BlockSpec((1, N_HEADS, tm, HEAD_PAD), lambda i, j: (i, 0, j, 0)),
            pl.BlockSpec((1, N_HEADS, V_DIM, tm), lambda i, j: (i, 0, 0, j)),
            tok(SSM_WIDTH), tok(2 * D_MODEL),
        ],
        out_shape=[
            jax.ShapeDtypeStruct((b, N_HEADS, HEAD_PAD, s), _BF16),
            jax.ShapeDtypeStruct((b, N_HEADS, s, HEAD_PAD), _BF16),
            jax.ShapeDtypeStruct((b, N_HEADS, V_DIM, s), _BF16),
            jax.ShapeDtypeStruct((b, s, SSM_WIDTH), _BF16),
            jax.ShapeDtypeStruct((b, s, 2 * D_MODEL), _BF16),
        ],
        compiler_params=_params(("parallel", "parallel")),
        name="inproj",
    )(x, mod6, g1, win, qag, wuq, kvag, wukv, qng, kng, ctab, stab)


def _attn_kernel(qt_ref, k_ref, vt_ref, kc_ref, vtc_ref, o_ref, s_a, s_b, c_a, c_b, p_a, p_b,
                 acc_ref, m_ref, al_ref, *, tiles):
    heads = range(ATTN_HEADS)
    n_kt = len(tiles)
    even = (s_a, c_a, p_a)
    odd = (s_b, c_b, p_b)

    def scores(t, j, dst):
        ctx, off, size = tiles[t]
        keys = (kc_ref if ctx else k_ref)[0, j, off:off + size, :]
        s = jnp.dot(keys, qt_ref[0, j], preferred_element_type=_F32).astype(_BF16)
        dst[0][j, :size] = s
        dst[1][j] = jnp.max(s, axis=0, keepdims=True).astype(_F32)

    def weighted(t, j, p_in):
        ctx, off, size = tiles[t]
        vt = (vtc_ref if ctx else vt_ref)[0, j, :, off:off + size]
        lhs = jnp.concatenate([vt, jnp.ones((ONES_ROWS, size), _BF16)], axis=0)
        pv = jnp.dot(lhs, p_in[j, :size], preferred_element_type=_F32)
        acc_ref[j] = acc_ref[j] * al_ref[j] + pv

    def softmax(t, j, cur):
        size = tiles[t][2]
        m = m_ref[j]
        m_new = jnp.maximum(m, cur[1][j])
        al_ref[j] = jnp.exp2(m - m_new)
        m_ref[j] = m_new
        cur[2][j, :size] = jnp.exp2(cur[0][j, :size] - m_new.astype(_BF16))

    def step(t, cur, nxt):
        if t > 0:
            for j in heads:
                weighted(t - 1, j, nxt[2])
        if t + 1 < n_kt:
            for j in heads:
                scores(t + 1, j, nxt)
        for j in heads:
            softmax(t, j, cur)

    for j in heads:
        m_ref[j] = jnp.full(m_ref.shape[1:], -jnp.inf, _F32)
        al_ref[j] = jnp.ones(al_ref.shape[1:], _F32)
        acc_ref[j] = jnp.zeros(acc_ref.shape[1:], _F32)
        scores(0, j, even)

    for t in range(n_kt):
        if t % 2 == 0:
            step(t, even, odd)
        else:
            step(t, odd, even)
    p_last = p_a if n_kt % 2 else p_b
    outs = []
    for j in heads:
        weighted(n_kt - 1, j, p_last)
        acc = acc_ref[j]
        outs.append(acc[:V_DIM] / acc[V_DIM:V_DIM + 1])
    o_ref[0] = jnp.concatenate(outs, axis=0).T.astype(_BF16)


def _attn_call(qt, k, vt, k_c, vt_c, tq, tk):
    b, h, _, s = qt.shape
    n_ctx = k_c.shape[2]
    tkc = min(tk, n_ctx)
    tiles = tuple((False, o, tk) for o in range(0, s, tk)) + tuple((True, o, tkc) for o in range(0, n_ctx, tkc))
    nh = ATTN_HEADS
    group = lambda rows, cols: pl.BlockSpec((1, nh, rows, cols), lambda i, p, j: (i, p, 0, 0))
    return pl.pallas_call(
        functools.partial(_attn_kernel, tiles=tiles),
        grid=(b, h // nh, s // tq),
        in_specs=[
            pl.BlockSpec((1, nh, HEAD_PAD, tq), lambda i, p, j: (i, p, 0, j)),
            group(s, HEAD_PAD), group(V_DIM, s), group(n_ctx, HEAD_PAD), group(V_DIM, n_ctx),
        ],
        out_specs=pl.BlockSpec((1, tq, nh * V_DIM), lambda i, p, j: (i, j, p)),
        out_shape=jax.ShapeDtypeStruct((b, s, h * V_DIM), _BF16),
        scratch_shapes=[
            pltpu.VMEM((nh, tk, tq), _BF16), pltpu.VMEM((nh, tk, tq), _BF16),
            pltpu.VMEM((nh, 1, tq), _F32), pltpu.VMEM((nh, 1, tq), _F32),
            pltpu.VMEM((nh, tk, tq), _BF16), pltpu.VMEM((nh, tk, tq), _BF16),
            pltpu.VMEM((nh, V_DIM + ONES_ROWS, tq), _F32),
            pltpu.VMEM((nh, 1, tq), _F32), pltpu.VMEM((nh, 1, tq), _F32),
        ],
        compiler_params=_params(("parallel", "parallel", "arbitrary")),
        name="attn",
    )(qt, k, vt, k_c, vt_c)


_SB = SSM_BLOCK * SSM_GROUP
_SN = SSM_BLOCK * SSM_STATE


def _cmul(ar, ai, br, bi):
    return ar * br - ai * bi, ar * bi + ai * br


def _ssm_powers(lre, lim, ldt):
    dt = jnp.exp(ldt)
    mag = jnp.exp(lre * dt)
    ar = mag * jnp.cos(lim * dt)
    ai = mag * jnp.sin(lim * dt)
    nr = ar - 1.0
    den = lre * lre + lim * lim
    cfr = (nr * lre + ai * lim) / den
    cfi = (ai * lre - nr * lim) / den
    pr = [jnp.ones_like(ar)]
    pi = [jnp.zeros_like(ar)]
    for _ in range(SSM_CHUNK):
        r, i = _cmul(pr[-1], pi[-1], ar, ai)
        pr.append(r)
        pi.append(i)
    return pr, pi, cfr, cfi


def _same_group():
    row = lax.broadcasted_iota(jnp.int32, (_SB, _SN), 0) // SSM_GROUP
    col = lax.broadcasted_iota(jnp.int32, (_SB, _SN), 1) // SSM_STATE
    return row == col


def _ssmtab_kernel(lre_ref, lim_ref, ldt_ref, mre_ref, mim_ref, dsk_ref, *outs, what):
    same = _same_group()
    zero = jnp.zeros((_SB, _SN), _F32)
    pw = [_ssm_powers(lre_ref[d, 0], lim_ref[d, 0], ldt_ref[d, 0]) for d in range(2)]

    def input_matrix(d, bre, bim):
        _, _, cfr, cfi = pw[d]
        r, i = _cmul(bre, bim, cfr, cfi)
        return jnp.where(same, r, zero), jnp.where(same, i, zero)

    if what == "w":
        (wt_ref,) = outs
        for d in range(2):
            pr, pi = pw[d][:2]
            bbr, bbi = input_matrix(d, mre_ref[0], mim_ref[0])
            for s in range(SSM_CHUNK):
                k = s if d else SSM_CHUNK - 1 - s
                r, i = _cmul(bbr, bbi, pr[k], pi[k])
                wt_ref[d, 0, s * _SB:(s + 1) * _SB, :_SN] = r.astype(_BF16)
                wt_ref[d, 0, s * _SB:(s + 1) * _SB, _SN:] = i.astype(_BF16)
    elif what == "r":
        r_ref, a_ref = outs
        for d in range(2):
            pr, pi = pw[d][:2]
            cr = jnp.where(same, mre_ref[d, 0], zero)
            ci = jnp.where(same, mim_ref[d, 0], zero)
            for t in range(SSM_CHUNK):
                k = SSM_CHUNK - t if d else t + 1
                r, i = _cmul(cr, ci, pr[k], pi[k])
                r_ref[d, 0, :_SN, t * _SB:(t + 1) * _SB] = r.T.astype(_BF16)
                r_ref[d, 0, _SN:, t * _SB:(t + 1) * _SB] = (-i).T.astype(_BF16)
            a_ref[0, 2 * d:2 * d + 1, :] = pr[SSM_CHUNK]
            a_ref[0, 2 * d + 1:2 * d + 2, :] = pi[SSM_CHUNK]
    else:
        cre_ref, cim_ref, tt_ref, lag_ref = outs
        dn = (((1,), (1,)), ((), ()))
        for d in range(2):
            pr, pi = pw[d][:2]
            bbr, bbi = input_matrix(d, mre_ref[0], mim_ref[0])
            cr = jnp.where(same, cre_ref[d, 0], zero)
            ci = jnp.where(same, cim_ref[d, 0], zero)
            for k in range(SSM_CHUNK):
                r, i = _cmul(bbr, bbi, pr[k], pi[k])
                kk = (lax.dot_general(r, cr, dn, precision=_HI, preferred_element_type=_F32)
                      - lax.dot_general(i, ci, dn, precision=_HI, preferred_element_type=_F32))
                lag_ref[d * SSM_CHUNK + k] = kk
        row = lax.broadcasted_iota(jnp.int32, (_SB, _SB), 0)
        col = lax.broadcasted_iota(jnp.int32, (_SB, _SB), 1)
        lag_ref[0] = lag_ref[0] + lag_ref[SSM_CHUNK] + jnp.where(row == col, dsk_ref[0], 0.0)
        for s in range(SSM_CHUNK):
            for t in range(SSM_CHUNK):
                k = t - s if t >= s else SSM_CHUNK + s - t
                tt_ref[0, s * _SB:(s + 1) * _SB, t * _SB:(t + 1) * _SB] = lag_ref[k].astype(_BF16)


def _ssmtab_call(what, lre, lim, ldt, mre, mim, dsk, cre=None, cim=None):
    nb = mre.shape[-3]
    row = pl.BlockSpec((2, 1, 1, _SN), lambda j: (0, j, 0, 0))
    mat1 = pl.BlockSpec((1, _SB, _SN), lambda j: (j, 0, 0))
    mat2 = pl.BlockSpec((2, 1, _SB, _SN), lambda j: (0, j, 0, 0))
    dspec = pl.BlockSpec((1, 1, _SB), lambda j: (j, 0, 0))
    l = SSM_CHUNK
    scratch = []
    if what == "w":
        in_specs, args = [row, row, row, mat1, mat1, dspec], (lre, lim, ldt, mre, mim, dsk)
        out_specs = [pl.BlockSpec((2, 1, l * _SB, 2 * _SN), lambda j: (0, j, 0, 0))]
        out_shape = [jax.ShapeDtypeStruct((2, nb, l * _SB, 2 * _SN), _BF16)]
    elif what == "r":
        in_specs, args = [row, row, row, mat2, mat2, dspec], (lre, lim, ldt, mre, mim, dsk)
        out_specs = [pl.BlockSpec((2, 1, 2 * _SN, l * _SB), lambda j: (0, j, 0, 0)),
                     pl.BlockSpec((1, 4, _SN), lambda j: (j, 0, 0))]
        out_shape = [jax.ShapeDtypeStruct((2, nb, 2 * _SN, l * _SB), _BF16),
                     jax.ShapeDtypeStruct((nb, 4, _SN), _F32)]
    else:
        in_specs, args = [row, row, row, mat1, mat1, dspec, mat2, mat2], (lre, lim, ldt, mre, mim, dsk, cre, cim)
        out_specs = [pl.BlockSpec((1, l * _SB, l * _SB), lambda j: (j, 0, 0))]
        out_shape = [jax.ShapeDtypeStruct((nb, l * _SB, l * _SB), _BF16)]
        scratch = [pltpu.VMEM((2 * l, _SB, _SB), _F32)]
    return pl.pallas_call(
        functools.partial(_ssmtab_kernel, what=what),
        grid=(nb,),
        in_specs=in_specs,
        out_specs=out_specs,
        out_shape=out_shape,
        scratch_shapes=scratch,
        compiler_params=_params(("parallel",)),
        name="ssmtab_" + what,
    )(*args)


def _ssm_tables(lam_re_f, lam_im_f, log_dt_f, c_re_f, c_im_f,
                lam_re_b, lam_im_b, log_dt_b, c_re_b, c_im_b, b_re, b_im, d_skip):
    g, n, c, gb = SSM_GROUPS, SSM_STATE, SSM_GROUP, SSM_BLOCK
    nb = g // gb
    rows = lambda f, b: jnp.stack([f, b]).reshape(2, nb, 1, gb * n)
    lre = rows(lam_re_f, lam_re_b)
    lim = rows(lam_im_f, lam_im_b)
    ldt = rows(jnp.broadcast_to(log_dt_f[:, None], (g, n)), jnp.broadcast_to(log_dt_b[:, None], (g, n)))
    tiled = lambda m: jnp.tile(m.reshape(nb, gb * c, n), (1, 1, gb))
    btre = tiled(b_re.transpose(0, 2, 1))
    btim = tiled(b_im.transpose(0, 2, 1))
    cre = jnp.stack([tiled(c_re_f), tiled(c_re_b)])
    cim = jnp.stack([tiled(c_im_f), tiled(c_im_b)])
    dsk = d_skip.reshape(nb, 1, gb * c)
    (wt,) = _ssmtab_call("w", lre, lim, ldt, btre, btim, dsk)
    r, a = _ssmtab_call("r", lre, lim, ldt, cre, cim, dsk)
    (tt,) = _ssmtab_call("t", lre, lim, ldt, btre, btim, dsk, cre, cim)
    return wt, tt, r, a


def _ssm_scan_kernel(uf_ref, ub_ref, wtf_ref, wtb_ref, a_ref, init_ref, xf_ref, xb_ref, fin_ref,
                     ef, eb, st, *, segc):
    i = pl.program_id(1)
    half = SSM_BLOCK * SSM_STATE

    @pl.when(i == 0)
    def _():
        st[...] = init_ref[0]

    ef[...] = jnp.dot(uf_ref[...], wtf_ref[0, 0], preferred_element_type=_F32)
    eb[...] = jnp.dot(ub_ref[...], wtb_ref[0, 0], preferred_element_type=_F32)
    a = a_ref[0]
    afr, afi, abr, abi = (jnp.broadcast_to(a[k:k + 1, :], (SSM_ROWS, half)) for k in range(4))

    def step(c, carry):
        fr, fi, br, bi = carry
        rf = pl.ds(pl.multiple_of(c * SSM_ROWS, SSM_ROWS), SSM_ROWS)
        rb = pl.ds(pl.multiple_of((segc - 1 - c) * SSM_ROWS, SSM_ROWS), SSM_ROWS)
        xf_ref[0, rf, :] = jnp.concatenate([fr, fi], axis=-1)
        xb_ref[0, rb, :] = jnp.concatenate([br, bi], axis=-1)
        e_f = ef[rf, :]
        e_b = eb[rb, :]
        nfr = afr * fr - afi * fi + e_f[:, :half]
        nfi = afr * fi + afi * fr + e_f[:, half:]
        nbr = abr * br - abi * bi + e_b[:, :half]
        nbi = abr * bi + abi * br + e_b[:, half:]
        return nfr, nfi, nbr, nbi

    fin = lax.fori_loop(0, segc, step, tuple(st[k] for k in range(4)))
    for k in range(4):
        st[k] = fin[k]
        fin_ref[0, k] = fin[k]


def _ssm_scan_call(ucat, wt, a, init):
    rows, width = ucat.shape
    nb = wt.shape[1]
    kw = width // nb
    half = SSM_BLOCK * SSM_STATE
    nc = rows // SSM_ROWS
    segc = min(nc, SSM_SEG_CHUNKS)
    nseg = nc // segc
    seg_rows = segc * SSM_ROWS
    f32 = lambda *shape: jax.ShapeDtypeStruct(shape, _F32)
    return pl.pallas_call(
        functools.partial(_ssm_scan_kernel, segc=segc),
        grid=(nb, nseg),
        in_specs=[
            pl.BlockSpec((seg_rows, kw), lambda j, i: (i, j)),
            pl.BlockSpec((seg_rows, kw), lambda j, i: (nseg - 1 - i, j)),
            pl.BlockSpec((1, 1, kw, 2 * half), lambda j, i: (0, j, 0, 0)),
            pl.BlockSpec((1, 1, kw, 2 * half), lambda j, i: (1, j, 0, 0)),
            pl.BlockSpec((1, 4, half), lambda j, i: (j, 0, 0)),
            pl.BlockSpec((1, 4, SSM_ROWS, half), lambda j, i: (j, 0, 0, 0)),
        ],
        out_specs=[
            pl.BlockSpec((1, seg_rows, 2 * half), lambda j, i: (j, i, 0)),
            pl.BlockSpec((1, seg_rows, 2 * half), lambda j, i: (j, nseg - 1 - i, 0)),
            pl.BlockSpec((1, 4, SSM_ROWS, half), lambda j, i: (j, 0, 0, 0)),
        ],
        out_shape=[f32(nb, rows, 2 * half), f32(nb, rows, 2 * half), f32(nb, 4, SSM_ROWS, half)],
        scratch_shapes=[pltpu.VMEM((seg_rows, 2 * half), _F32), pltpu.VMEM((seg_rows, 2 * half), _F32),
                        pltpu.VMEM((4, SSM_ROWS, half), _F32)],
        compiler_params=_params(("parallel", "arbitrary")),
        name="ssm_scan",
    )(ucat, ucat, wt, wt, a, init)


def _chunk_powers(ar, ai, rows, descending):
    tr = jnp.ones((SSM_ROWS,) + ar.shape[1:], _F32)
    ti = jnp.zeros_like(tr)
    sr, si = ar, ai
    size = SSM_ROWS
    while size < rows:
        hr, hi = _cmul(tr, ti, sr, si)
        tr = jnp.concatenate([hr, tr] if descending else [tr, hr], axis=0)
        ti = jnp.concatenate([hi, ti] if descending else [ti, hi], axis=0)
        sr, si = _cmul(sr, si, sr, si)
        size *= 2
    return tr, ti, sr, si


def _tile_rows(v, rows):
    while v.shape[0] < rows:
        v = jnp.concatenate([v, v], axis=0)
    return v


def _ssm_out_kernel(u_ref, xf_ref, xb_ref, tt_ref, rf_ref, rb_ref, a_ref, fix_ref, y_ref, *, split):
    xf = xf_ref[0]
    xb = xb_ref[0]
    if split:
        i = pl.program_id(1)
        ni = pl.num_programs(1)
        rows = xf.shape[0]
        a = a_ref[0]
        fix = fix_ref[0]

        def missing(ar, ai, fr, fi, descending, tiles_before):
            pr, pi, sr, si = _chunk_powers(ar, ai, rows, descending)
            one = (jnp.ones_like(ar), jnp.zeros_like(ar))
            hr, hi = lax.fori_loop(0, tiles_before, lambda _, c: _cmul(c[0], c[1], sr, si), one)
            pr, pi = _cmul(pr, pi, hr, hi)
            cr, ci = _cmul(pr, pi, _tile_rows(fr, rows), _tile_rows(fi, rows))
            return jnp.concatenate([cr, ci], axis=-1)

        xf = xf + missing(a[0:1], a[1:2], fix[0], fix[1], False, i)
        xb = xb + missing(a[2:3], a[3:4], fix[2], fix[3], True, ni - 1 - i)
    y = jnp.dot(u_ref[...], tt_ref[0], preferred_element_type=_F32)
    y = y + jnp.dot(xf.astype(_BF16), rf_ref[0, 0], preferred_element_type=_F32)
    y = y + jnp.dot(xb.astype(_BF16), rb_ref[0, 0], preferred_element_type=_F32)
    y_ref[...] = y.astype(_BF16)


def _ssm_out_call(ucat, xf, xb, tt, r, a, fix, split):
    rows, width = ucat.shape
    nb = tt.shape[0]
    kw = width // nb
    sw = xf.shape[2]
    tr = _pick_tile(rows, 512)
    once = pl.Buffered(1)
    return pl.pallas_call(
        functools.partial(_ssm_out_kernel, split=split),
        grid=(nb, rows // tr),
        in_specs=[
            pl.BlockSpec((tr, kw), lambda j, i: (i, j)),
            pl.BlockSpec((1, tr, sw), lambda j, i: (j, i, 0)),
            pl.BlockSpec((1, tr, sw), lambda j, i: (j, i, 0)),
            pl.BlockSpec((1, kw, kw), lambda j, i: (j, 0, 0), pipeline_mode=once),
            pl.BlockSpec((1, 1, sw, kw), lambda j, i: (0, j, 0, 0), pipeline_mode=once),
            pl.BlockSpec((1, 1, sw, kw), lambda j, i: (1, j, 0, 0), pipeline_mode=once),
            pl.BlockSpec((1, 4, sw // 2), lambda j, i: (j, 0, 0)),
            pl.BlockSpec((1, 4, SSM_ROWS, sw // 2), lambda j, i: (j, 0, 0, 0)),
        ],
        out_specs=pl.BlockSpec((tr, kw), lambda j, i: (i, j)),
        out_shape=jax.ShapeDtypeStruct((rows, width), _BF16),
        compiler_params=_params(("parallel", "parallel")),
        name="ssm_out",
    )(ucat, xf, xb, tt, r, r, a, fix)


def _to_chunks(u, split):
    b, s, _ = u.shape
    nb = SSM_GROUPS // SSM_BLOCK
    lane = SSM_BLOCK * SSM_GROUP
    if split:
        nc = s // SSM_CHUNK // 2
        uc = u.reshape(b, 2, nc, SSM_CHUNK, nb, lane).transpose(2, 1, 0, 4, 3, 5)
    else:
        nc = s // SSM_CHUNK
        uc = u.reshape(b, nc, SSM_CHUNK, nb, lane).transpose(1, 0, 3, 2, 4)
        uc = jnp.pad(uc, ((0, 0), (0, SSM_ROWS - b), (0, 0), (0, 0), (0, 0)))
    return uc.reshape(nc * SSM_ROWS, SSM_CHUNK * SSM_WIDTH)


def _from_chunks(y, b):
    rows, _ = y.shape
    nc = rows // SSM_ROWS
    nb = SSM_GROUPS // SSM_BLOCK
    y = y.reshape(nc, 2, b, nb, SSM_CHUNK, SSM_BLOCK * SSM_GROUP)
    return y.transpose(2, 1, 0, 4, 3, 5).reshape(b, 2 * nc * SSM_CHUNK, SSM_WIDTH)


def _mix_kernel(x_ref, mod_ref, o_ref, y_ref, sg_ref, wo_ref, wglu_ref, wout_ref, x1_ref):
    a = jnp.dot(o_ref[0], wo_ref[...], preferred_element_type=_F32)
    yg = jax.nn.gelu(y_ref[0].astype(_F32))
    glu = jnp.dot(yg.astype(_BF16), wglu_ref[...], preferred_element_type=_F32)
    s = glu[:, :D_MODEL] * jax.nn.sigmoid(glu[:, D_MODEL:])
    sg = sg_ref[0].astype(_F32)
    merged = sg[:, :D_MODEL] * a + sg[:, D_MODEL:] * s
    out = jnp.dot(merged.astype(_BF16), wout_ref[...], preferred_element_type=_F32)
    g1 = mod_ref[0][2:3, :]
    x1_ref[0] = x_ref[0] + g1 * out


def _mix_call(x, mod6, o, y, sg, wo, wglu, wout, tm):
    b, s, d = x.shape
    tok = lambda w: pl.BlockSpec((1, tm, w), lambda i, j: (i, j, 0))
    return pl.pallas_call(
        _mix_kernel,
        grid=(b, s // tm),
        in_specs=[tok(d), pl.BlockSpec((1, 6, d), lambda i, j: (i, 0, 0)),
                  tok(o.shape[2]), tok(y.shape[2]), tok(sg.shape[2]),
                  _const_spec(wo.shape), _const_spec(wglu.shape), _const_spec(wout.shape)],
        out_specs=tok(d),
        out_shape=jax.ShapeDtypeStruct((b, s, d), _F32),
        compiler_params=_params(("parallel", "parallel")),
        name="mix",
    )(x, mod6, o, y, sg, wo, wglu, wout)


def _ffn_kernel(x_ref, prev_ref, next_ref, mod_ref, g2_ref, wup_ref, cw_ref, cb_ref, wdn_ref,
                out_ref, acc_ref, *, n_chunks):
    j = pl.program_id(1)
    nj = pl.num_programs(1)
    tm = x_ref.shape[1]
    mod = mod_ref[0]
    sh2 = mod[3:4, :]
    sc2 = mod[4:5, :]
    g2 = mod[5:6, :]
    gain = g2_ref[...]

    def prenorm(v):
        return ((_rms(v, D_MODEL) * gain) * (1.0 + sc2) + sh2).astype(_BF16)

    x = x_ref[0]
    h = prenorm(jnp.concatenate([x, prev_ref[0], next_ref[0]], axis=0))
    has_prev = (j > 0).astype(_F32)
    has_next = (j < nj - 1).astype(_F32)
    row = lax.broadcasted_iota(jnp.int32, (8, FFN_CHUNK), 0)

    def up(ci):
        lo = ci * FFN_CHUNK
        return tuple(jnp.dot(h, wup_ref[:, o + lo:o + lo + FFN_CHUNK], preferred_element_type=_F32)
                     for o in (0, FFN_HIDDEN))

    def conv(pe, lo):
        p = pe[:tm]
        before = pltpu.roll(p, 1, 0)
        after = pltpu.roll(p, tm - 1, 0)
        first = jnp.where(row == 0, pe[tm + 7:tm + 8] * has_prev, before[:8])
        last = jnp.where(row == 7, pe[tm + 8:tm + 9] * has_next, after[tm - 8:])
        before = jnp.concatenate([first, before[8:]], axis=0)
        after = jnp.concatenate([after[:tm - 8], last], axis=0)
        cw = cw_ref[:, lo:lo + FFN_CHUNK]
        return before * cw[0:1, :] + p * cw[1:2, :] + after * cw[2:3, :] + cb_ref[:, lo:lo + FFN_CHUNK]

    def gated(ci, pe):
        val = conv(pe[0], ci * FFN_CHUNK)
        gate = conv(pe[1], FFN_HIDDEN + ci * FFN_CHUNK)
        return (gate * jax.nn.sigmoid(gate) * val).astype(_BF16)

    pe = up(0)
    for ci in range(n_chunks):
        nxt = up(ci + 1) if ci + 1 < n_chunks else None
        act = gated(ci, pe)
        dn = jnp.dot(act, wdn_ref[ci * FFN_CHUNK:(ci + 1) * FFN_CHUNK, :], preferred_element_type=_F32)
        if ci == 0:
            acc_ref[...] = dn
        else:
            acc_ref[...] += dn
        pe = nxt
    out_ref[0] = x + g2 * acc_ref[...]


def _ffn_call(x1, mod6, g2, wup, cw, cb, wdn, tm):
    b, s, d = x1.shape
    n_chunks = wdn.shape[0] // FFN_CHUNK
    hb = tm // 8
    last = s // 8 - 1
    return pl.pallas_call(
        functools.partial(_ffn_kernel, n_chunks=n_chunks),
        grid=(b, s // tm),
        in_specs=[
            pl.BlockSpec((1, tm, d), lambda i, j: (i, j, 0)),
            pl.BlockSpec((1, 8, d), lambda i, j: (i, jnp.maximum(j * hb - 1, 0), 0)),
            pl.BlockSpec((1, 8, d), lambda i, j: (i, jnp.minimum((j + 1) * hb, last), 0)),
            pl.BlockSpec((1, 6, d), lambda i, j: (i, 0, 0)),
            _const_spec(g2.shape), _const_spec(wup.shape), _const_spec(cw.shape),
            _const_spec(cb.shape), _const_spec(wdn.shape),
        ],
        out_specs=pl.BlockSpec((1, tm, d), lambda i, j: (i, j, 0)),
        out_shape=jax.ShapeDtypeStruct((b, s, d), _F32),
        scratch_shapes=[pltpu.VMEM((tm, d), _F32)],
        compiler_params=_params(("parallel", "arbitrary")),
        name="ffn",
    )(x1, x1, x1, mod6, g2, wup, cw, cb, wdn)


def _rope_tables(s):
    rows = s // GRID_W
    row = jnp.repeat(jnp.arange(rows), GRID_W)
    col = jnp.tile(jnp.arange(GRID_W), rows)
    pairs = QK_ROPE // 4
    freqs = ROPE_THETA ** (-jnp.arange(pairs, dtype=_F32) / pairs)
    ang = jnp.concatenate([row[:, None] * freqs, col[:, None] * freqs], axis=-1)
    cos, sin = jnp.cos(ang), jnp.sin(ang)
    z = lambda w: jnp.zeros((s, w), _F32)
    ctab = jnp.concatenate([jnp.ones((s, QK_NOPE), _F32), cos, cos, z(HEAD_PAD - QK_DIM)], axis=-1)
    stab = jnp.concatenate([z(QK_NOPE), -sin, sin, z(HEAD_PAD - QK_DIM)], axis=-1)
    return ctab, stab


def _identity_tables(s):
    ctab = jnp.concatenate([jnp.ones((s, QK_DIM), _F32), jnp.zeros((s, HEAD_PAD - QK_DIM), _F32)], axis=-1)
    return ctab, jnp.zeros((s, HEAD_PAD), _F32)


def _swap_rope_halves(w):
    half = QK_ROPE // 2
    return jnp.concatenate([jnp.zeros_like(w[..., :QK_NOPE]), w[..., QK_NOPE + half:QK_DIM],
                            w[..., QK_NOPE:QK_NOPE + half]], axis=-1)


def _pick_tile(n, pref):
    t = min(n, pref)
    while n % t:
        t //= 2
    return t


def kernel(x, c, ctx, c_ctx, w_mod, b_mod, norm1_g, norm2_g, w_in, q_a_g, w_uq, kv_a_g, w_ukv, q_norm_g, k_norm_g, w_o_attn, lam_re_f, lam_im_f, log_dt_f, c_re_f, c_im_f, lam_re_b, lam_im_b, log_dt_b, c_re_b, c_im_b, b_re, b_im, d_skip, w_glu, w_out, w_up, conv_w, conv_b, w_down):
    b, s, d = x.shape
    n_ctx = ctx.shape[1]
    depth = w_mod.shape[0]
    assert depth == 1, "context update between layers is not implemented"
    l = 0

    wi = w_in[l]
    o1, o2, o3, o4 = Q_LORA, Q_LORA + KV_LORA, Q_LORA + KV_LORA + QK_ROPE, Q_LORA + KV_LORA + QK_ROPE + SSM_WIDTH
    zc = lambda w: jnp.zeros((d, w), wi.dtype)
    half = QK_ROPE // 2
    win = jnp.concatenate([wi[:, :o1], wi[:, o1:o2], wi[:, o3:o4], wi[:, o4:],
                           zc(QK_NOPE), wi[:, o2:o3], zc(HEAD_PAD - QK_DIM),
                           zc(QK_NOPE), wi[:, o2 + half:o3], wi[:, o2:o2 + half], zc(HEAD_PAD - QK_DIM)],
                          axis=-1).astype(_BF16)
    pad_head = lambda w: jnp.pad(w, [(0, 0)] * (w.ndim - 1) + [(0, HEAD_PAD - QK_DIM)])
    wq3 = w_uq[l].reshape(Q_LORA, N_HEADS, QK_DIM)
    wuq = jnp.concatenate([pad_head(wq3).reshape(Q_LORA, N_HEADS * HEAD_PAD),
                           pad_head(_swap_rope_halves(wq3)).reshape(Q_LORA, N_HEADS * HEAD_PAD)],
                          axis=-1).astype(_BF16)
    wkv = w_ukv[l].reshape(KV_LORA, N_HEADS, QK_NOPE + V_DIM)
    wk = jnp.pad(wkv[:, :, :QK_NOPE], ((0, 0), (0, 0), (0, HEAD_PAD - QK_NOPE))).reshape(KV_LORA, N_HEADS * HEAD_PAD)
    wv = wkv[:, :, QK_NOPE:].reshape(KV_LORA, N_HEADS * V_DIM)
    wukv = jnp.concatenate([wk, wv], axis=-1).astype(_BF16)
    padg = lambda g: jnp.stack([pad_head(g), pad_head(_swap_rope_halves(g))], axis=0)
    qng, kng = padg(q_norm_g[l]), padg(k_norm_g[l])
    g1 = norm1_g[l].reshape(1, d)
    g2 = norm2_g[l].reshape(1, d)
    qag = q_a_g[l].reshape(1, Q_LORA)
    kvag = kv_a_g[l].reshape(1, KV_LORA)
    wup = w_up[l].astype(_BF16)
    cw = conv_w[l]
    cb = conv_b[l].reshape(1, 2 * FFN_HIDDEN)
    wdn = w_down[l].astype(_BF16)
    wo = w_o_attn[l].astype(_BF16)
    wglu = w_glu[l].astype(_BF16)
    wout = w_out[l].astype(_BF16)

    cc = jnp.concatenate([c, c_ctx[None, :], jnp.zeros((8 - b - 1, d), c.dtype)], axis=0)
    mod = _mod_call(cc, w_mod[l], b_mod[l])
    mod_lat = mod[:b].reshape(b, 6, d)
    mod_ctx = jnp.broadcast_to(mod[b].reshape(1, 6, d), (b, 6, d))

    tm = _pick_tile(s, 512)
    tmc = _pick_tile(n_ctx, 512)
    shared = (g1, win, qag, wuq, kvag, wukv, qng, kng)
    qt, k, vt, u, sg = _inproj_call(x, mod_lat, *shared, *_rope_tables(s), tm)
    _, k_c, vt_c, u_c, _ = _inproj_call(ctx, mod_ctx, *shared, *_identity_tables(n_ctx), tmc)

    wt, tt, r, a = _ssm_tables(lam_re_f[l], lam_im_f[l], log_dt_f[l], c_re_f[l], c_im_f[l],
                               lam_re_b[l], lam_im_b[l], log_dt_b[l], c_re_b[l], c_im_b[l],
                               b_re[l], b_im[l], d_skip[l])
    assert 2 * b == SSM_ROWS, "the latent scan packs two sequence halves of 4 batch rows into 8 sublanes"
    zero_state = jnp.zeros((SSM_GROUPS // SSM_BLOCK, 2, b, SSM_BLOCK * SSM_STATE), _F32)
    zero_init = jnp.concatenate([zero_state, zero_state], axis=2)
    _, _, fin_c = _ssm_scan_call(_to_chunks(u_c, False), wt, a, jnp.concatenate([zero_init, zero_init], axis=1))
    init = jnp.concatenate([jnp.concatenate([fin_c[:, 0:2, :b], zero_state], axis=2),
                            jnp.concatenate([zero_state, fin_c[:, 2:4, :b]], axis=2)], axis=1)
    ucat = _to_chunks(u, True)
    xf, xb, fin = _ssm_scan_call(ucat, wt, a, init)
    fix = jnp.concatenate([jnp.concatenate([zero_state, fin[:, 0:2, :b]], axis=2),
                           jnp.concatenate([fin[:, 2:4, b:], zero_state], axis=2)], axis=1)
    y = _from_chunks(_ssm_out_call(ucat, xf, xb, tt, r, a, fix, True), b)

    tq = _pick_tile(s, 512)
    tk = _pick_tile(s, ATTN_TK)
    o = _attn_call(qt, k, vt, k_c, vt_c, tq, tk)

    x1 = _mix_call(x, mod_lat, o, y, sg, wo, wglu, wout, tm)
    return _ffn_call(x1, mod_lat, g2, wup, cw, cb, wdn, tm)
```

```python
import functools
import math

import jax
import jax.numpy as jnp
from jax import lax
from jax.experimental import pallas as pl
from jax.experimental.pallas import tpu as pltpu

D_MODEL = 1024
GRID_W = 64
N_HEADS = 8
QK_NOPE = 64
QK_ROPE = 32
QK_DIM = QK_NOPE + QK_ROPE
V_DIM = 64
Q_LORA = 384
KV_LORA = 256
ROPE_THETA = 10000.0
SSM_WIDTH = 512
SSM_GROUP = 16
SSM_GROUPS = SSM_WIDTH // SSM_GROUP
SSM_STATE = 64
FFN_HIDDEN = 2816
EPS = 1e-6

LANES = 128
HEAD_PAD = LANES
ONES_ROWS = 16
INPROJ_SUB = 128
ATTN_TK = 256
ATTN_HEADS = 2
SSM_CHUNK = 16
SSM_ROWS = 8
SSM_BLOCK = 8
SSM_SEG_CHUNKS = 64
FFN_CHUNK = 256
FFN_DOWN_GROUP = 6
VMEM_LIMIT = 56 * 1024 * 1024

_HI = lax.Precision.HIGHEST
_F32 = jnp.float32
_BF16 = jnp.bfloat16


def _params(sem):
    return pltpu.CompilerParams(dimension_semantics=sem, vmem_limit_bytes=VMEM_LIMIT)


def _const_spec(shape):
    nd = len(shape)
    return pl.BlockSpec(shape, lambda *_: (0,) * nd)


def _rms(v, width):
    return v * lax.rsqrt(jnp.sum(v * v, axis=-1, keepdims=True) * (1.0 / width) + EPS)


def _mod_kernel(c_ref, w_ref, b_ref, o_ref):
    c = c_ref[...]
    s = c * jax.nn.sigmoid(c)
    o_ref[...] = jnp.dot(s, w_ref[...], precision=_HI, preferred_element_type=_F32) + b_ref[...]


def _mod_call(cc, w_mod, b_mod):
    rows, d = cc.shape
    n = w_mod.shape[1]
    tn = 1024
    return pl.pallas_call(
        _mod_kernel,
        grid=(n // tn,),
        in_specs=[
            pl.BlockSpec((rows, d), lambda j: (0, 0)),
            pl.BlockSpec((d, tn), lambda j: (0, j)),
            pl.BlockSpec((1, tn), lambda j: (0, j)),
        ],
        out_specs=pl.BlockSpec((rows, tn), lambda j: (0, j)),
        out_shape=jax.ShapeDtypeStruct((rows, n), _F32),
        compiler_params=_params(("arbitrary",)),
        name="mod",
    )(cc, w_mod, b_mod.reshape(1, n))


_C_CQ = 0
_C_CKV = _C_CQ + Q_LORA
_C_U = _C_CKV + KV_LORA
_C_GL = _C_U + SSM_WIDTH
_C_KR = _C_GL + 2 * D_MODEL
_C_KRP = _C_KR + LANES
_IN_W = _C_KRP + LANES
_QW = N_HEADS * HEAD_PAD


def _inproj_kernel(x_ref, mod_ref, g1_ref, win_ref, qag_ref, wuq_ref, kvag_ref, wukv_ref,
                   qng_ref, kng_ref, ctab_ref, stab_ref,
                   qt_ref, k_ref, vt_ref, u_ref, sg_ref, *, sub):
    tm = x_ref.shape[1]
    mod = mod_ref[0]
    sh1 = mod[0:1, :]
    sc1 = 1.0 + mod[1:2, :]
    g1 = g1_ref[...]
    qg = qng_ref[...] * (QK_DIM ** -0.5 * math.log2(math.e))
    kg = kng_ref[...]

    def project(i):
        x = x_ref[0, i * sub:(i + 1) * sub, :]
        h = (_rms(x, D_MODEL) * g1) * sc1 + sh1
        return jnp.dot(h.astype(_BF16), win_ref[...], preferred_element_type=_F32)

    def expand(i, proj):
        rows = slice(i * sub, (i + 1) * sub)
        u_ref[0, rows, :] = proj[:, _C_U:_C_GL].astype(_BF16)
        sg_ref[0, rows, :] = jax.nn.sigmoid(proj[:, _C_GL:_C_KR]).astype(_BF16)
        cq = _rms(proj[:, _C_CQ:_C_CKV], Q_LORA) * qag_ref[...]
        qall = jnp.dot(cq.astype(_BF16), wuq_ref[...], preferred_element_type=_F32)
        ckv = _rms(proj[:, _C_CKV:_C_U], KV_LORA) * kvag_ref[...]
        kvall = jnp.dot(ckv.astype(_BF16), wukv_ref[...], preferred_element_type=_F32)
        return qall, kvall, proj[:, _C_KR:_C_KRP], proj[:, _C_KRP:_IN_W]

    def heads(i, qall, kvall, kr, krp):
        rows = slice(i * sub, (i + 1) * sub)
        ctab = ctab_ref[rows, :]
        stab = stab_ref[rows, :]
        qc = qg[0:1] * ctab
        qs = qg[1:2] * stab
        kc = kg[0:1] * ctab
        k_rot = krp * (kg[1:2] * stab)
        for hd in range(N_HEADS):
            lo = hd * HEAD_PAD
            qh = qall[:, lo:lo + HEAD_PAD]
            nq = lax.rsqrt(jnp.sum(qh * qh, axis=-1, keepdims=True) * (1.0 / QK_DIM) + EPS)
            qr = (qh * qc + qall[:, _QW + lo:_QW + lo + HEAD_PAD] * qs) * nq
            qt_ref[0, hd, :, rows] = qr.T.astype(_BF16)
            kh = kvall[:, lo:lo + HEAD_PAD] + kr
            nk = lax.rsqrt(jnp.sum(kh * kh, axis=-1, keepdims=True) * (1.0 / QK_DIM) + EPS)
            k_ref[0, hd, rows, :] = ((kh * kc + k_rot) * nk).astype(_BF16)
        for hp in range(N_HEADS // 2):
            lo = _QW + hp * LANES
            vt = kvall[:, lo:lo + LANES].T.astype(_BF16)
            vt_ref[0, 2 * hp, :, rows] = vt[:V_DIM]
            vt_ref[0, 2 * hp + 1, :, rows] = vt[V_DIM:]

    n_sub = tm // sub
    proj = project(0)
    for i in range(n_sub):
        nxt = project(i + 1) if i + 1 < n_sub else None
        heads(i, *expand(i, proj))
        proj = nxt


def _inproj_call(x, mod6, g1, win, qag, wuq, kvag, wukv, qng, kng, ctab, stab, tm):
    b, s, d = x.shape
    grid = (b, s // tm)
    sub = min(tm, INPROJ_SUB)
    tok = lambda w: pl.BlockSpec((1, tm, w), lambda i, j: (i, j, 0))
    tab = pl.BlockSpec((tm, LANES), lambda i, j: (j, 0))
    return pl.pallas_call(
        functools.partial(_inproj_kernel, sub=sub),
        grid=grid,
        in_specs=[
            tok(d),
            pl.BlockSpec((1, 6, d), lambda i, j: (i, 0, 0)),
            _const_spec(g1.shape), _const_spec(win.shape), _const_spec(qag.shape),
            _const_spec(wuq.shape), _const_spec(kvag.shape), _const_spec(wukv.shape),
            _const_spec(qng.shape), _const_spec(kng.shape),
            tab, tab,
        ],
        out_specs=[
            pl.BlockSpec((1, N_HEADS, HEAD_PAD, tm), lambda i, j: (i, 0, 0, j)),
            pl.BlockSpec((1, N_HEADS, tm, HEAD_PAD), lambda i, j: (i, 0, j, 0)),
            pl.BlockSpec((1, N_HEADS, V_DIM, tm), lambda i, j: (i, 0, 0, j)),
            tok(SSM_WIDTH), tok(2 * D_MODEL),
        ],
        out_shape=[
            jax.ShapeDtypeStruct((b, N_HEADS, HEAD_PAD, s), _BF16),
            jax.ShapeDtypeStruct((b, N_HEADS, s, HEAD_PAD), _BF16),
            jax.ShapeDtypeStruct((b, N_HEADS, V_DIM, s), _BF16),
            jax.ShapeDtypeStruct((b, s, SSM_WIDTH), _BF16),
            jax.ShapeDtypeStruct((b, s, 2 * D_MODEL), _BF16),
        ],
        compiler_params=_params(("parallel", "parallel")),
        name="inproj",
    )(x, mod6, g1, win, qag, wuq, kvag, wukv, qng, kng, ctab, stab)


def _attn_kernel(qt_ref, k_ref, vt_ref, kc_ref, vtc_ref, o_ref, s_a, s_b, c_a, c_b, p_a, p_b,
                 acc_ref, m_ref, al_ref, *, tiles):
    heads = range(ATTN_HEADS)
    n_kt = len(tiles)
    even = (s_a, c_a, p_a)
    odd = (s_b, c_b, p_b)

    def scores(t, j, dst):
        ctx, off, size = tiles[t]
        keys = (kc_ref if ctx else k_ref)[0, j, off:off + size, :]
        s = jnp.dot(keys, qt_ref[0, j], preferred_element_type=_F32).astype(_BF16)
        dst[0][j, :size] = s
        dst[1][j] = jnp.max(s, axis=0, keepdims=True).astype(_F32)

    def weighted(t, j, p_in):
        ctx, off, size = tiles[t]
        vt = (vtc_ref if ctx else vt_ref)[0, j, :, off:off + size]
        lhs = jnp.concatenate([vt, jnp.ones((ONES_ROWS, size), _BF16)], axis=0)
        pv = jnp.dot(lhs, p_in[j, :size], preferred_element_type=_F32)
        acc_ref[j] = acc_ref[j] * al_ref[j] + pv

    def softmax(t, j, cur):
        size = tiles[t][2]
        m = m_ref[j]
        m_new = jnp.maximum(m, cur[1][j])
        al_ref[j] = jnp.exp2(m - m_new)
        m_ref[j] = m_new
        cur[2][j, :size] = jnp.exp2(cur[0][j, :size] - m_new.astype(_BF16))

    def step(t, cur, nxt):
        if t > 0:
            for j in heads:
                weighted(t - 1, j, nxt[2])
        if t + 1 < n_kt:
            for j in heads:
                scores(t + 1, j, nxt)
        for j in heads:
            softmax(t, j, cur)

    for j in heads:
        m_ref[j] = jnp.full(m_ref.shape[1:], -jnp.inf, _F32)
        al_ref[j] = jnp.ones(al_ref.shape[1:], _F32)
        acc_ref[j] = jnp.zeros(acc_ref.shape[1:], _F32)
        scores(0, j, even)

    for t in range(n_kt):
        if t % 2 == 0:
            step(t, even, odd)
        else:
            step(t, odd, even)
    p_last = p_a if n_kt % 2 else p_b
    outs = []
    for j in heads:
        weighted(n_kt - 1, j, p_last)
        acc = acc_ref[j]
        outs.append(acc[:V_DIM] / acc[V_DIM:V_DIM + 1])
    o_ref[0] = jnp.concatenate(outs, axis=0).T.astype(_BF16)


def _attn_call(qt, k, vt, k_c, vt_c, tq, tk):
    b, h, _, s = qt.shape
    n_ctx = k_c.shape[2]
    tkc = min(tk, n_ctx)
    tiles = tuple((False, o, tk) for o in range(0, s, tk)) + tuple((True, o, tkc) for o in range(0, n_ctx, tkc))
    nh = ATTN_HEADS
    group = lambda rows, cols: pl.BlockSpec((1, nh, rows, cols), lambda i, p, j: (i, p, 0, 0))
    return pl.pallas_call(
        functools.partial(_attn_kernel, tiles=tiles),
        grid=(b, h // nh, s // tq),
        in_specs=[
            pl.BlockSpec((1, nh, HEAD_PAD, tq), lambda i, p, j: (i, p, 0, j)),
            group(s, HEAD_PAD), group(V_DIM, s), group(n_ctx, HEAD_PAD), group(V_DIM, n_ctx),
        ],
        out_specs=pl.BlockSpec((1, tq, nh * V_DIM), lambda i, p, j: (i, j, p)),
        out_shape=jax.ShapeDtypeStruct((b, s, h * V_DIM), _BF16),
        scratch_shapes=[
            pltpu.VMEM((nh, tk, tq), _BF16), pltpu.VMEM((nh, tk, tq), _BF16),
            pltpu.VMEM((nh, 1, tq), _F32), pltpu.VMEM((nh, 1, tq), _F32),
            pltpu.VMEM((nh, tk, tq), _BF16), pltpu.VMEM((nh, tk, tq), _BF16),
            pltpu.VMEM((nh, V_DIM + ONES_ROWS, tq), _F32),
            pltpu.VMEM((nh, 1, tq), _F32), pltpu.VMEM((nh, 1, tq), _F32),
        ],
        compiler_params=_params(("parallel", "parallel", "arbitrary")),
        name="attn",
    )(qt, k, vt, k_c, vt_c)


_SB = SSM_BLOCK * SSM_GROUP
_SN = SSM_BLOCK * SSM_STATE


def _cmul(ar, ai, br, bi):
    return ar * br - ai * bi, ar * bi + ai * br


def _ssm_powers(lre, lim, ldt):
    dt = jnp.exp(ldt)
    mag = jnp.exp(lre * dt)
    ar = mag * jnp.cos(lim * dt)
    ai = mag * jnp.sin(lim * dt)
    nr = ar - 1.0
    den = lre * lre + lim * lim
    cfr = (nr * lre + ai * lim) / den
    cfi = (ai * lre - nr * lim) / den
    pr = [jnp.ones_like(ar)]
    pi = [jnp.zeros_like(ar)]
    for _ in range(SSM_CHUNK):
        r, i = _cmul(pr[-1], pi[-1], ar, ai)
        pr.append(r)
        pi.append(i)
    return pr, pi, cfr, cfi


def _same_group():
    row = lax.broadcasted_iota(jnp.int32, (_SB, _SN), 0) // SSM_GROUP
    col = lax.broadcasted_iota(jnp.int32, (_SB, _SN), 1) // SSM_STATE
    return row == col


def _ssmtab_kernel(lre_ref, lim_ref, ldt_ref, mre_ref, mim_ref, dsk_ref, *outs, what):
    same = _same_group()
    zero = jnp.zeros((_SB, _SN), _F32)
    pw = [_ssm_powers(lre_ref[d, 0], lim_ref[d, 0], ldt_ref[d, 0]) for d in range(2)]

    def input_matrix(d, bre, bim):
        _, _, cfr, cfi = pw[d]
        r, i = _cmul(bre, bim, cfr, cfi)
        return jnp.where(same, r, zero), jnp.where(same, i, zero)

    if what == "w":
        (wt_ref,) = outs
        for d in range(2):
            pr, pi = pw[d][:2]
            bbr, bbi = input_matrix(d, mre_ref[0], mim_ref[0])
            for s in range(SSM_CHUNK):
                k = s if d else SSM_CHUNK - 1 - s
                r, i = _cmul(bbr, bbi, pr[k], pi[k])
                wt_ref[d, 0, s * _SB:(s + 1) * _SB, :_SN] = r.astype(_BF16)
                wt_ref[d, 0, s * _SB:(s + 1) * _SB, _SN:] = i.astype(_BF16)
    elif what == "r":
        r_ref, a_ref = outs
        for d in range(2):
            pr, pi = pw[d][:2]
            cr = jnp.where(same, mre_ref[d, 0], zero)
            ci = jnp.where(same, mim_ref[d, 0], zero)
            for t in range(SSM_CHUNK):
                k = SSM_CHUNK - t if d else t + 1
                r, i = _cmul(cr, ci, pr[k], pi[k])
                r_ref[d, 0, :_SN, t * _SB:(t + 1) * _SB] = r.T.astype(_BF16)
                r_ref[d, 0, _SN:, t * _SB:(t + 1) * _SB] = (-i).T.astype(_BF16)
            a_ref[0, 2 * d:2 * d + 1, :] = pr[SSM_CHUNK]
            a_ref[0, 2 * d + 1:2 * d + 2, :] = pi[SSM_CHUNK]
    else:
        cre_ref, cim_ref, tt_ref, lag_ref = outs
        dn = (((1,), (1,)), ((), ()))
        for d in range(2):
            pr, pi = pw[d][:2]
            bbr, bbi = input_matrix(d, mre_ref[0], mim_ref[0])
            cr = jnp.where(same, cre_ref[d, 0], zero)
            ci = jnp.where(same, cim_ref[d, 0], zero)
            for k in range(SSM_CHUNK):
                r, i = _cmul(bbr, bbi, pr[k], pi[k])
                kk = (lax.dot_general(r, cr, dn, precision=_HI, preferred_element_type=_F32)
                      - lax.dot_general(i, ci, dn, precision=_HI, preferred_element_type=_F32))
                lag_ref[d * SSM_CHUNK + k] = kk
        row = lax.broadcasted_iota(jnp.int32, (_SB, _SB), 0)
        col = lax.broadcasted_iota(jnp.int32, (_SB, _SB), 1)
        lag_ref[0] = lag_ref[0] + lag_ref[SSM_CHUNK] + jnp.where(row == col, dsk_ref[0], 0.0)
        for s in range(SSM_CHUNK):
            for t in range(SSM_CHUNK):
                k = t - s if t >= s else SSM_CHUNK + s - t
                tt_ref[0, s * _SB:(s + 1) * _SB, t * _SB:(t + 1) * _SB] = lag_ref[k].astype(_BF16)


def _ssmtab_call(what, lre, lim, ldt, mre, mim, dsk, cre=None, cim=None):
    nb = mre.shape[-3]
    row = pl.BlockSpec((2, 1, 1, _SN), lambda j: (0, j, 0, 0))
    mat1 = pl.BlockSpec((1, _SB, _SN), lambda j: (j, 0, 0))
    mat2 = pl.BlockSpec((2, 1, _SB, _SN), lambda j: (0, j, 0, 0))
    dspec = pl.BlockSpec((1, 1, _SB), lambda j: (j, 0, 0))
    l = SSM_CHUNK
    scratch = []
    if what == "w":
        in_specs, args = [row, row, row, mat1, mat1, dspec], (lre, lim, ldt, mre, mim, dsk)
        out_specs = [pl.BlockSpec((2, 1, l * _SB, 2 * _SN), lambda j: (0, j, 0, 0))]
        out_shape = [jax.ShapeDtypeStruct((2, nb, l * _SB, 2 * _SN), _BF16)]
    elif what == "r":
        in_specs, args = [row, row, row, mat2, mat2, dspec], (lre, lim, ldt, mre, mim, dsk)
        out_specs = [pl.BlockSpec((2, 1, 2 * _SN, l * _SB), lambda j: (0, j, 0, 0)),
                     pl.BlockSpec((1, 4, _SN), lambda j: (j, 0, 0))]
        out_shape = [jax.ShapeDtypeStruct((2, nb, 2 * _SN, l * _SB), _BF16),
                     jax.ShapeDtypeStruct((nb, 4, _SN), _F32)]
    else:
        in_specs, args = [row, row, row, mat1, mat1, dspec, mat2, mat2], (lre, lim, ldt, mre, mim, dsk, cre, cim)
        out_specs = [pl.BlockSpec((1, l * _SB, l * _SB), lambda j: (j, 0, 0))]
        out_shape = [jax.ShapeDtypeStruct((nb, l * _SB, l * _SB), _BF16)]
        scratch = [pltpu.VMEM((2 * l, _SB, _SB), _F32)]
    return pl.pallas_call(
        functools.partial(_ssmtab_kernel, what=what),
        grid=(nb,),
        in_specs=in_specs,
        out_specs=out_specs,
        out_shape=out_shape,
        scratch_shapes=scratch,
        compiler_params=_params(("parallel",)),
        name="ssmtab_" + what,
    )(*args)


def _ssm_tables(lam_re_f, lam_im_f, log_dt_f, c_re_f, c_im_f,
                lam_re_b, lam_im_b, log_dt_b, c_re_b, c_im_b, b_re, b_im, d_skip):
    g, n, c, gb = SSM_GROUPS, SSM_STATE, SSM_GROUP, SSM_BLOCK
    nb = g // gb
    rows = lambda f, b: jnp.stack([f, b]).reshape(2, nb, 1, gb * n)
    lre = rows(lam_re_f, lam_re_b)
    lim = rows(lam_im_f, lam_im_b)
    ldt = rows(jnp.broadcast_to(log_dt_f[:, None], (g, n)), jnp.broadcast_to(log_dt_b[:, None], (g, n)))
    tiled = lambda m: jnp.tile(m.reshape(nb, gb * c, n), (1, 1, gb))
    btre = tiled(b_re.transpose(0, 2, 1))
    btim = tiled(b_im.transpose(0, 2, 1))
    cre = jnp.stack([tiled(c_re_f), tiled(c_re_b)])
    cim = jnp.stack([tiled(c_im_f), tiled(c_im_b)])
    dsk = d_skip.reshape(nb, 1, gb * c)
    (wt,) = _ssmtab_call("w", lre, lim, ldt, btre, btim, dsk)
    r, a = _ssmtab_call("r", lre, lim, ldt, cre, cim, dsk)
    (tt,) = _ssmtab_call("t", lre, lim, ldt, btre, btim, dsk, cre, cim)
    return wt, tt, r, a


def _ssm_scan_kernel(uf_ref, ub_ref, wtf_ref, wtb_ref, a_ref, init_ref, xf_ref, xb_ref, fin_ref,
                     ef, eb, st, *, segc):
    i = pl.program_id(1)
    half = SSM_BLOCK * SSM_STATE

    @pl.when(i == 0)
    def _():
        st[...] = init_ref[0]

    ef[...] = jnp.dot(uf_ref[...], wtf_ref[0, 0], preferred_element_type=_F32)
    eb[...] = jnp.dot(ub_ref[...], wtb_ref[0, 0], preferred_element_type=_F32)
    a = a_ref[0]
    afr, afi, abr, abi = (jnp.broadcast_to(a[k:k + 1, :], (SSM_ROWS, half)) for k in range(4))

    def step(c, carry):
        fr, fi, br, bi = carry
        rf = pl.ds(pl.multiple_of(c * SSM_ROWS, SSM_ROWS), SSM_ROWS)
        rb = pl.ds(pl.multiple_of((segc - 1 - c) * SSM_ROWS, SSM_ROWS), SSM_ROWS)
        xf_ref[0, rf, :] = jnp.concatenate([fr, fi], axis=-1)
        xb_ref[0, rb, :] = jnp.concatenate([br, bi], axis=-1)
        e_f = ef[rf, :]
        e_b = eb[rb, :]
        nfr = afr * fr - afi * fi + e_f[:, :half]
        nfi = afr * fi + afi * fr + e_f[:, half:]
        nbr = abr * br - abi * bi + e_b[:, :half]
        nbi = abr * bi + abi * br + e_b[:, half:]
        return nfr, nfi, nbr, nbi

    fin = lax.fori_loop(0, segc, step, tuple(st[k] for k in range(4)))
    for k in range(4):
        st[k] = fin[k]
        fin_ref[0, k] = fin[k]


def _ssm_scan_call(ucat, wt, a, init):
    rows, width = ucat.shape
    nb = wt.shape[1]
    kw = width // nb
    half = SSM_BLOCK * SSM_STATE
    nc = rows // SSM_ROWS
    segc = min(nc, SSM_SEG_CHUNKS)
    nseg = nc // segc
    seg_rows = segc * SSM_ROWS
    f32 = lambda *shape: jax.ShapeDtypeStruct(shape, _F32)
    return pl.pallas_call(
        functools.partial(_ssm_scan_kernel, segc=segc),
        grid=(nb, nseg),
        in_specs=[
            pl.BlockSpec((seg_rows, kw), lambda j, i: (i, j)),
            pl.BlockSpec((seg_rows, kw), lambda j, i: (nseg - 1 - i, j)),
            pl.BlockSpec((1, 1, kw, 2 * half), lambda j, i: (0, j, 0, 0)),
            pl.BlockSpec((1, 1, kw, 2 * half), lambda j, i: (1, j, 0, 0)),
            pl.BlockSpec((1, 4, half), lambda j, i: (j, 0, 0)),
            pl.BlockSpec((1, 4, SSM_ROWS, half), lambda j, i: (j, 0, 0, 0)),
        ],
        out_specs=[
            pl.BlockSpec((1, seg_rows, 2 * half), lambda j, i: (j, i, 0)),
            pl.BlockSpec((1, seg_rows, 2 * half), lambda j, i: (j, nseg - 1 - i, 0)),
            pl.BlockSpec((1, 4, SSM_ROWS, half), lambda j, i: (j, 0, 0, 0)),
        ],
        out_shape=[f32(nb, rows, 2 * half), f32(nb, rows, 2 * half), f32(nb, 4, SSM_ROWS, half)],
        scratch_shapes=[pltpu.VMEM((seg_rows, 2 * half), _F32), pltpu.VMEM((seg_rows, 2 * half), _F32),
                        pltpu.VMEM((4, SSM_ROWS, half), _F32)],
        compiler_params=_params(("parallel", "arbitrary")),
        name="ssm_scan",
    )(ucat, ucat, wt, wt, a, init)


def _chunk_powers(ar, ai, rows, descending):
    tr = jnp.ones((SSM_ROWS,) + ar.shape[1:], _F32)
    ti = jnp.zeros_like(tr)
    sr, si = ar, ai
    size = SSM_ROWS
    while size < rows:
        hr, hi = _cmul(tr, ti, sr, si)
        tr = jnp.concatenate([hr, tr] if descending else [tr, hr], axis=0)
        ti = jnp.concatenate([hi, ti] if descending else [ti, hi], axis=0)
        sr, si = _cmul(sr, si, sr, si)
        size *= 2
    return tr, ti, sr, si


def _tile_rows(v, rows):
    while v.shape[0] < rows:
        v = jnp.concatenate([v, v], axis=0)
    return v


def _ssm_out_kernel(u_ref, xf_ref, xb_ref, tt_ref, rf_ref, rb_ref, a_ref, fix_ref, y_ref, *, split):
    xf = xf_ref[0]
    xb = xb_ref[0]
    if split:
        i = pl.program_id(1)
        ni = pl.num_programs(1)
        rows = xf.shape[0]
        a = a_ref[0]
        fix = fix_ref[0]

        def missing(ar, ai, fr, fi, descending, tiles_before):
            pr, pi, sr, si = _chunk_powers(ar, ai, rows, descending)
            one = (jnp.ones_like(ar), jnp.zeros_like(ar))
            hr, hi = lax.fori_loop(0, tiles_before, lambda _, c: _cmul(c[0], c[1], sr, si), one)
            pr, pi = _cmul(pr, pi, hr, hi)
            cr, ci = _cmul(pr, pi, _tile_rows(fr, rows), _tile_rows(fi, rows))
            return jnp.concatenate([cr, ci], axis=-1)

        xf = xf + missing(a[0:1], a[1:2], fix[0], fix[1], False, i)
        xb = xb + missing(a[2:3], a[3:4], fix[2], fix[3], True, ni - 1 - i)
    y = jnp.dot(u_ref[...], tt_ref[0], preferred_element_type=_F32)
    y = y + jnp.dot(xf.astype(_BF16), rf_ref[0, 0], preferred_element_type=_F32)
    y = y + jnp.dot(xb.astype(_BF16), rb_ref[0, 0], preferred_element_type=_F32)
    y_ref[...] = y.astype(_BF16)


def _ssm_out_call(ucat, xf, xb, tt, r, a, fix, split):
    rows, width = ucat.shape
    nb = tt.shape[0]
    kw = width // nb
    sw = xf.shape[2]
    tr = _pick_tile(rows, 512)
    once = pl.Buffered(1)
    return pl.pallas_call(
        functools.partial(_ssm_out_kernel, split=split),
        grid=(nb, rows // tr),
        in_specs=[
            pl.BlockSpec((tr, kw), lambda j, i: (i, j)),
            pl.BlockSpec((1, tr, sw), lambda j, i: (j, i, 0)),
            pl.BlockSpec((1, tr, sw), lambda j, i: (j, i, 0)),
            pl.BlockSpec((1, kw, kw), lambda j, i: (j, 0, 0), pipeline_mode=once),
            pl.BlockSpec((1, 1, sw, kw), lambda j, i: (0, j, 0, 0), pipeline_mode=once),
            pl.BlockSpec((1, 1, sw, kw), lambda j, i: (1, j, 0, 0), pipeline_mode=once),
            pl.BlockSpec((1, 4, sw // 2), lambda j, i: (j, 0, 0)),
            pl.BlockSpec((1, 4, SSM_ROWS, sw // 2), lambda j, i: (j, 0, 0, 0)),
        ],
        out_specs=pl.BlockSpec((tr, kw), lambda j, i: (i, j)),
        out_shape=jax.ShapeDtypeStruct((rows, width), _BF16),
        compiler_params=_params(("parallel", "parallel")),
        name="ssm_out",
    )(ucat, xf, xb, tt, r, r, a, fix)


def _to_chunks(u, split):
    b, s, _ = u.shape
    nb = SSM_GROUPS // SSM_BLOCK
    lane = SSM_BLOCK * SSM_GROUP
    if split:
        nc = s // SSM_CHUNK // 2
        uc = u.reshape(b, 2, nc, SSM_CHUNK, nb, lane).transpose(2, 1, 0, 4, 3, 5)
    else:
        nc = s // SSM_CHUNK
        uc = u.reshape(b, nc, SSM_CHUNK, nb, lane).transpose(1, 0, 3, 2, 4)
        uc = jnp.pad(uc, ((0, 0), (0, SSM_ROWS - b), (0, 0), (0, 0), (0, 0)))
    return uc.reshape(nc * SSM_ROWS, SSM_CHUNK * SSM_WIDTH)


def _from_chunks(y, b):
    rows, _ = y.shape
    nc = rows // SSM_ROWS
    nb = SSM_GROUPS // SSM_BLOCK
    y = y.reshape(nc, 2, b, nb, SSM_CHUNK, SSM_BLOCK * SSM_GROUP)
    return y.transpose(2, 1, 0, 4, 3, 5).reshape(b, 2 * nc * SSM_CHUNK, SSM_WIDTH)


def _mix_kernel(x_ref, mod_ref, o_ref, y_ref, sg_ref, wo_ref, wglu_ref, wout_ref, x1_ref):
    a = jnp.dot(o_ref[0], wo_ref[...], preferred_element_type=_F32)
    yg = jax.nn.gelu(y_ref[0].astype(_F32))
    glu = jnp.dot(yg.astype(_BF16), wglu_ref[...], preferred_element_type=_F32)
    s = glu[:, :D_MODEL] * jax.nn.sigmoid(glu[:, D_MODEL:])
    sg = sg_ref[0].astype(_F32)
    merged = sg[:, :D_MODEL] * a + sg[:, D_MODEL:] * s
    out = jnp.dot(merged.astype(_BF16), wout_ref[...], preferred_element_type=_F32)
    g1 = mod_ref[0][2:3, :]
    x1_ref[0] = x_ref[0] + g1 * out


def _mix_call(x, mod6, o, y, sg, wo, wglu, wout, tm):
    b, s, d = x.shape
    tok = lambda w: pl.BlockSpec((1, tm, w), lambda i, j: (i, j, 0))
    return pl.pallas_call(
        _mix_kernel,
        grid=(b, s // tm),
        in_specs=[tok(d), pl.BlockSpec((1, 6, d), lambda i, j: (i, 0, 0)),
                  tok(o.shape[2]), tok(y.shape[2]), tok(sg.shape[2]),
                  _const_spec(wo.shape), _const_spec(wglu.shape), _const_spec(wout.shape)],
        out_specs=tok(d),
        out_shape=jax.ShapeDtypeStruct((b, s, d), _F32),
        compiler_params=_params(("parallel", "parallel")),
        name="mix",
    )(x, mod6, o, y, sg, wo, wglu, wout)


def _ffn_kernel(x_ref, prev_ref, next_ref, mod_ref, g2_ref, wup_ref, cw_ref, cb_ref, wdn_ref,
                out_ref, acc_ref, *, n_chunks):
    j = pl.program_id(1)
    nj = pl.num_programs(1)
    tm = x_ref.shape[1]
    mod = mod_ref[0]
    sh2 = mod[3:4, :]
    sc2 = mod[4:5, :]
    g2 = mod[5:6, :]
    gain = g2_ref[...]

    def prenorm(v):
        return ((_rms(v, D_MODEL) * gain) * (1.0 + sc2) + sh2).astype(_BF16)

    x = x_ref[0]
    h = prenorm(jnp.concatenate([x, prev_ref[0], next_ref[0]], axis=0))
    has_prev = (j > 0).astype(_F32)
    has_next = (j < nj - 1).astype(_F32)
    row = lax.broadcasted_iota(jnp.int32, (8, FFN_CHUNK), 0)

    def up(ci):
        lo = ci * FFN_CHUNK
        return tuple(jnp.dot(h, wup_ref[:, o + lo:o + lo + FFN_CHUNK], preferred_element_type=_F32)
                     for o in (0, FFN_HIDDEN))

    def conv(pe, lo):
        p = pe[:tm]
        before = pltpu.roll(p, 1, 0)
        after = pltpu.roll(p, tm - 1, 0)
        first = jnp.where(row == 0, pe[tm + 7:tm + 8] * has_prev, before[:8])
        last = jnp.where(row == 7, pe[tm + 8:tm + 9] * has_next, after[tm - 8:])
        before = jnp.concatenate([first, before[8:]], axis=0)
        after = jnp.concatenate([after[:tm - 8], last], axis=0)
        cw = cw_ref[:, lo:lo + FFN_CHUNK]
        return before * cw[0:1, :] + p * cw[1:2, :] + after * cw[2:3, :] + cb_ref[:, lo:lo + FFN_CHUNK]

    def gated(ci, pe):
        val = conv(pe[0], ci * FFN_CHUNK)
        gate = conv(pe[1], FFN_HIDDEN + ci * FFN_CHUNK)
        return (gate * jax.nn.sigmoid(gate) * val).astype(_BF16)

    pe = up(0)
    acts = []
    for ci in range(n_chunks):
        nxt = up(ci + 1) if ci + 1 < n_chunks else None
        acts.append(gated(ci, pe))
        if len(acts) == FFN_DOWN_GROUP or ci == n_chunks - 1:
            lo = (ci + 1 - len(acts)) * FFN_CHUNK
            act = acts[0] if len(acts) == 1 else jnp.concatenate(acts, axis=1)
            dn = jnp.dot(act, wdn_ref[lo:(ci + 1) * FFN_CHUNK, :], preferred_element_type=_F32)
            if lo == 0:
                acc_ref[...] = dn
            else:
                acc_ref[...] += dn
            acts = []
        pe = nxt
    out_ref[0] = x + g2 * acc_ref[...]


def _ffn_call(x1, mod6, g2, wup, cw, cb, wdn, tm):
    b, s, d = x1.shape
    n_chunks = wdn.shape[0] // FFN_CHUNK
    hb = tm // 8
    last = s // 8 - 1
    return pl.pallas_call(
        functools.partial(_ffn_kernel, n_chunks=n_chunks),
        grid=(b, s // tm),
        in_specs=[
            pl.BlockSpec((1, tm, d), lambda i, j: (i, j, 0)),
            pl.BlockSpec((1, 8, d), lambda i, j: (i, jnp.maximum(j * hb - 1, 0), 0)),
            pl.BlockSpec((1, 8, d), lambda i, j: (i, jnp.minimum((j + 1) * hb, last), 0)),
            pl.BlockSpec((1, 6, d), lambda i, j: (i, 0, 0)),
            _const_spec(g2.shape), _const_spec(wup.shape), _const_spec(cw.shape),
            _const_spec(cb.shape), _const_spec(wdn.shape),
        ],
        out_specs=pl.BlockSpec((1, tm, d), lambda i, j: (i, j, 0)),
        out_shape=jax.ShapeDtypeStruct((b, s, d), _F32),
        scratch_shapes=[pltpu.VMEM((tm, d), _F32)],
        compiler_params=_params(("parallel", "arbitrary")),
        name="ffn",
    )(x1, x1, x1, mod6, g2, wup, cw, cb, wdn)


def _rope_tables(s):
    rows = s // GRID_W
    row = jnp.repeat(jnp.arange(rows), GRID_W)
    col = jnp.tile(jnp.arange(GRID_W), rows)
    pairs = QK_ROPE // 4
    freqs = ROPE_THETA ** (-jnp.arange(pairs, dtype=_F32) / pairs)
    ang = jnp.concatenate([row[:, None] * freqs, col[:, None] * freqs], axis=-1)
    cos, sin = jnp.cos(ang), jnp.sin(ang)
    z = lambda w: jnp.zeros((s, w), _F32)
    ctab = jnp.concatenate([jnp.ones((s, QK_NOPE), _F32), cos, cos, z(HEAD_PAD - QK_DIM)], axis=-1)
    stab = jnp.concatenate([z(QK_NOPE), -sin, sin, z(HEAD_PAD - QK_DIM)], axis=-1)
    return ctab, stab


def _identity_tables(s):
    ctab = jnp.concatenate([jnp.ones((s, QK_DIM), _F32), jnp.zeros((s, HEAD_PAD - QK_DIM), _F32)], axis=-1)
    return ctab, jnp.zeros((s, HEAD_PAD), _F32)


def _swap_rope_halves(w):
    half = QK_ROPE // 2
    return jnp.concatenate([jnp.zeros_like(w[..., :QK_NOPE]), w[..., QK_NOPE + half:QK_DIM],
                            w[..., QK_NOPE:QK_NOPE + half]], axis=-1)


def _pick_tile(n, pref):
    t = min(n, pref)
    while n % t:
        t //= 2
    return t


def kernel(x, c, ctx, c_ctx, w_mod, b_mod, norm1_g, norm2_g, w_in, q_a_g, w_uq, kv_a_g, w_ukv, q_norm_g, k_norm_g, w_o_attn, lam_re_f, lam_im_f, log_dt_f, c_re_f, c_im_f, lam_re_b, lam_im_b, log_dt_b, c_re_b, c_im_b, b_re, b_im, d_skip, w_glu, w_out, w_up, conv_w, conv_b, w_down):
    b, s, d = x.shape
    n_ctx = ctx.shape[1]
    depth = w_mod.shape[0]
    assert depth == 1, "context update between layers is not implemented"
    l = 0

    wi = w_in[l]
    o1, o2, o3, o4 = Q_LORA, Q_LORA + KV_LORA, Q_LORA + KV_LORA + QK_ROPE, Q_LORA + KV_LORA + QK_ROPE + SSM_WIDTH
    zc = lambda w: jnp.zeros((d, w), wi.dtype)
    half = QK_ROPE // 2
    win = jnp.concatenate([wi[:, :o1], wi[:, o1:o2], wi[:, o3:o4], wi[:, o4:],
                           zc(QK_NOPE), wi[:, o2:o3], zc(HEAD_PAD - QK_DIM),
                           zc(QK_NOPE), wi[:, o2 + half:o3], wi[:, o2:o2 + half], zc(HEAD_PAD - QK_DIM)],
                          axis=-1).astype(_BF16)
    pad_head = lambda w: jnp.pad(w, [(0, 0)] * (w.ndim - 1) + [(0, HEAD_PAD - QK_DIM)])
    wq3 = w_uq[l].reshape(Q_LORA, N_HEADS, QK_DIM)
    wuq = jnp.concatenate([pad_head(wq3).reshape(Q_LORA, N_HEADS * HEAD_PAD),
                           pad_head(_swap_rope_halves(wq3)).reshape(Q_LORA, N_HEADS * HEAD_PAD)],
                          axis=-1).astype(_BF16)
    wkv = w_ukv[l].reshape(KV_LORA, N_HEADS, QK_NOPE + V_DIM)
    wk = jnp.pad(wkv[:, :, :QK_NOPE], ((0, 0), (0, 0), (0, HEAD_PAD - QK_NOPE))).reshape(KV_LORA, N_HEADS * HEAD_PAD)
    wv = wkv[:, :, QK_NOPE:].reshape(KV_LORA, N_HEADS * V_DIM)
    wukv = jnp.concatenate([wk, wv], axis=-1).astype(_BF16)
    padg = lambda g: jnp.stack([pad_head(g), pad_head(_swap_rope_halves(g))], axis=0)
    qng, kng = padg(q_norm_g[l]), padg(k_norm_g[l])
    g1 = norm1_g[l].reshape(1, d)
    g2 = norm2_g[l].reshape(1, d)
    qag = q_a_g[l].reshape(1, Q_LORA)
    kvag = kv_a_g[l].reshape(1, KV_LORA)
    wup = w_up[l].astype(_BF16)
    cw = conv_w[l]
    cb = conv_b[l].reshape(1, 2 * FFN_HIDDEN)
    wdn = w_down[l].astype(_BF16)
    wo = w_o_attn[l].astype(_BF16)
    wglu = w_glu[l].astype(_BF16)
    wout = w_out[l].astype(_BF16)

    cc = jnp.concatenate([c, c_ctx[None, :], jnp.zeros((8 - b - 1, d), c.dtype)], axis=0)
    mod = _mod_call(cc, w_mod[l], b_mod[l])
    mod_lat = mod[:b].reshape(b, 6, d)
    mod_ctx = jnp.broadcast_to(mod[b].reshape(1, 6, d), (b, 6, d))

    tm = _pick_tile(s, 512)
    tmc = _pick_tile(n_ctx, 512)
    shared = (g1, win, qag, wuq, kvag, wukv, qng, kng)
    qt, k, vt, u, sg = _inproj_call(x, mod_lat, *shared, *_rope_tables(s), tm)
    _, k_c, vt_c, u_c, _ = _inproj_call(ctx, mod_ctx, *shared, *_identity_tables(n_ctx), tmc)

    wt, tt, r, a = _ssm_tables(lam_re_f[l], lam_im_f[l], log_dt_f[l], c_re_f[l], c_im_f[l],
                               lam_re_b[l], lam_im_b[l], log_dt_b[l], c_re_b[l], c_im_b[l],
                               b_re[l], b_im[l], d_skip[l])
    assert 2 * b == SSM_ROWS, "the latent scan packs two sequence halves of 4 batch rows into 8 sublanes"
    zero_state = jnp.zeros((SSM_GROUPS // SSM_BLOCK, 2, b, SSM_BLOCK * SSM_STATE), _F32)
    zero_init = jnp.concatenate([zero_state, zero_state], axis=2)
    _, _, fin_c = _ssm_scan_call(_to_chunks(u_c, False), wt, a, jnp.concatenate([zero_init, zero_init], axis=1))
    init = jnp.concatenate([jnp.concatenate([fin_c[:, 0:2, :b], zero_state], axis=2),
                            jnp.concatenate([zero_state, fin_c[:, 2:4, :b]], axis=2)], axis=1)
    ucat = _to_chunks(u, True)
    xf, xb, fin = _ssm_scan_call(ucat, wt, a, init)
    fix = jnp.concatenate([jnp.concatenate([zero_state, fin[:, 0:2, :b]], axis=2),
                           jnp.concatenate([fin[:, 2:4, b:], zero_state], axis=2)], axis=1)
    y = _from_chunks(_ssm_out_call(ucat, xf, xb, tt, r, a, fix, True), b)

    tq = _pick_tile(s, 512)
    tk = _pick_tile(s, ATTN_TK)
    o = _attn_call(qt, k, vt, k_c, vt_c, tq, tk)

    x1 = _mix_call(x, mod_lat, o, y, sg, wo, wglu, wout, tm)
    return _ffn_call(x1, mod_lat, g2, wup, cw, cb, wdn, tm)
```
